```python
import functools
import jax, jax.numpy as jnp
from jax import lax
import numpy as np

D_MODEL = 1024
BATCH = 8
SEQ = 2048
DEPTH = 4
DEC_BATCH = 128
DEC_SEQ = 4
PAST_LEN = 2048
PAGE_SIZE = 128

N_HEADS = 16
N_KV = 4
HEADS_PER_KV = N_HEADS // N_KV
HEAD_DIM = 64
ROT_DIM = HEAD_DIM // 4
ROPE_THETA = 500000.0
CMP_BLOCK = 32
CMP_STRIDE = 16
CMP_HIDDEN = 128
SLC_BLOCK = 64
N_SELECT = 8
WINDOW = 512
Q_BLOCK = 64
ATTN_SCALE = HEAD_DIM ** -0.5
D_RNN = D_MODEL
N_RNN_BLOCKS = 16
RNN_BW = D_RNN // N_RNN_BLOCKS
CONV_W = 4
RG_C = 8.0
D_POOL = D_MODEL
POOL_WINDOWS = (2, 4, 8, 16)
POOL_GW = D_POOL // len(POOL_WINDOWS)
POOL_BUF = max(POOL_WINDOWS) - 1
D_FF = 2816
ATTN_W = N_HEADS * HEAD_DIM
KV_W = 3 * 2 * N_KV * HEAD_DIM
IN_W = 2 * D_RNN + D_POOL + ATTN_W + KV_W + 3 * N_HEADS + 3 * D_MODEL
ALPHA = (2 * DEPTH) ** 0.25
BETA = (8 * DEPTH) ** -0.25
LN_EPS = 1e-5

kernel_name = 'hybrid_rglru_pool_nsa_macaron_step'


def layer_norm(x, g, b):
    xf = x.astype(jnp.float32)
    mu = xf.mean(-1, keepdims=True)
    var = jnp.square(xf - mu).mean(-1, keepdims=True)
    y = (xf - mu) * lax.rsqrt(var + LN_EPS) * g.astype(jnp.float32) + b.astype(jnp.float32)
    return y.astype(x.dtype)


def adaln(c, w, b):
    m = jax.nn.silu(c) @ w + b
    return m.reshape(c.shape[0], 9, 1, D_MODEL)


def rotary(x, pos):
    half = ROT_DIM // 2
    inv = ROPE_THETA ** (-jnp.arange(half, dtype=jnp.float32) * (2.0 / ROT_DIM))
    ang = (pos.astype(jnp.float32)[:, None] * inv).reshape((pos.shape[0],) + (1,) * (x.ndim - 3) + (half,))
    cos, sin = jnp.cos(ang), jnp.sin(ang)
    x1 = x[..., :half].astype(jnp.float32)
    x2 = x[..., half:ROT_DIM].astype(jnp.float32)
    rot = jnp.concatenate([x1 * cos - x2 * sin, x2 * cos + x1 * sin], axis=-1).astype(x.dtype)
    return jnp.concatenate([rot, x[..., ROT_DIM:]], axis=-1)


def rope_kv(kv, pos):
    return jnp.stack([rotary(kv[:, :, 0], pos), kv[:, :, 1]], axis=2)


def masked_softmax(s, mask):
    s = jnp.where(mask, s, -jnp.inf)
    m = jnp.max(s, axis=-1, keepdims=True)
    m = jnp.where(jnp.isfinite(m), m, 0.0)
    e = jnp.exp(s - m)
    return e / jnp.maximum(e.sum(-1, keepdims=True), 1e-30)


def linear_scan(a, b, h0):
    b = b.at[:, 0].add(a[:, 0] * h0)
    def combine(left, right):
        return (left[0] * right[0], right[0] * left[1] + right[1])
    return lax.associative_scan(combine, (a, b), axis=1)[1]


def rglru(u, gate_pre, conv_buf, h0, conv_w, conv_b, gate_w, gate_b, lam):
    B, T, _ = u.shape
    ext = jnp.concatenate([conv_buf.astype(u.dtype), u], axis=1)
    xc = conv_b
    for k in range(CONV_W):
        xc = xc + ext[:, k:k + T] * conv_w[k]
    xb = xc.reshape(B, T, N_RNN_BLOCKS, RNN_BW)
    gts = jnp.einsum('btnk,cnkj->cbtnj', xb, gate_w).reshape(2, B, T, D_RNN) + gate_b[:, None, None, :]
    r = jax.nn.sigmoid(gts[0].astype(jnp.float32))
    i = jax.nn.sigmoid(gts[1].astype(jnp.float32))
    log_a = -RG_C * r * jax.nn.softplus(-lam.astype(jnp.float32))
    a = jnp.exp(log_a)
    b = jnp.sqrt(-jnp.expm1(2.0 * log_a)) * (i * xc.astype(jnp.float32))
    h = linear_scan(a, b, h0.astype(jnp.float32))
    y = h.astype(u.dtype) * jax.nn.gelu(gate_pre)
    return y, ext[:, T:], h[:, -1].astype(u.dtype)


def pool_mix(u, buf, pos0, mix_w, scale):
    B, T, _ = u.shape
    ext = jnp.concatenate([buf.astype(u.dtype), u], axis=1)
    cs = jnp.cumsum(jnp.concatenate([jnp.zeros((B, 1, D_POOL), jnp.float32), ext.astype(jnp.float32)], axis=1), axis=1)
    hi = cs[:, POOL_BUF + 1:POOL_BUF + 1 + T]
    pos = pos0 + jnp.arange(T, dtype=jnp.int32)
    outs = []
    for g, w in enumerate(POOL_WINDOWS):
        sl = slice(g * POOL_GW, (g + 1) * POOL_GW)
        lo = cs[:, POOL_BUF + 1 - w:POOL_BUF + 1 - w + T, sl]
        cnt = jnp.minimum(pos + 1, w).astype(jnp.float32)[None, :, None]
        d = ((hi[:, :, sl] - lo) / cnt - u[:, :, sl].astype(jnp.float32)).astype(u.dtype)
        outs.append(d @ mix_w[g])
    y = jnp.concatenate(outs, axis=-1) * scale
    return y, ext[:, T:]


def compress(kv, w1, b1, w2, pe):
    B, T = kv.shape[0], kv.shape[1]
    R = CMP_BLOCK // CMP_STRIDE
    n_chunk = T // CMP_STRIDE
    nb = n_chunk - R + 1
    ch = kv[:, :n_chunk * CMP_STRIDE].reshape(B, n_chunk, CMP_STRIDE, 2, N_KV, HEAD_DIM)
    ch = ch.transpose(0, 1, 3, 4, 2, 5).reshape(B, n_chunk, 2, N_KV, CMP_STRIDE * HEAD_DIM)
    w1r = w1.reshape(2, R, CMP_STRIDE * HEAD_DIM, CMP_HIDDEN)
    proj = jnp.einsum('bncgf,crfh->rbncgh', ch, w1r)
    pre = proj[0, :, 0:nb]
    for r in range(1, R):
        pre = pre + proj[r, :, r:r + nb]
    pe_term = jnp.einsum('cf,cfh->ch', pe.reshape(2, CMP_BLOCK * HEAD_DIM), w1) + b1
    hid = jax.nn.gelu(pre + pe_term[None, None, :, None, :])
    out = jnp.einsum('bncgh,chd->bncgd', hid, w2)
    return out[:, :, 0], out[:, :, 1]


def to_blocks(kv):
    B, T = kv.shape[0], kv.shape[1]
    ns = -(-T // SLC_BLOCK)
    kv = jnp.pad(kv, ((0, 0), (0, ns * SLC_BLOCK - T), (0, 0), (0, 0), (0, 0)))
    kv = kv.reshape(B, ns, SLC_BLOCK, 2, N_KV, HEAD_DIM).transpose(3, 0, 4, 1, 2, 5)
    return kv[0], kv[1]


def nsa_core(q, qpos, kc, vc, ks, vs, kw, vw, wpos, gates):
    B, Q = q.shape[0], q.shape[1]
    nb, ns = kc.shape[1], ks.shape[2]
    s_c = jnp.einsum('bqghd,bkgd->bghqk', q, kc).astype(jnp.float32) * ATTN_SCALE
    blk_end = jnp.arange(nb) * CMP_STRIDE + CMP_BLOCK - 1
    p_c = masked_softmax(s_c, blk_end[None, :] <= qpos[:, None])
    o_c = jnp.einsum('bghqk,bkgd->bqghd', p_c.astype(q.dtype), vc)
    imp = p_c.sum(axis=2)
    ci = jnp.arange(nb)[:, None] * CMP_STRIDE
    sj = jnp.arange(ns)[None, :] * SLC_BLOCK
    overlap = ((ci <= sj + SLC_BLOCK - 1) & (ci + CMP_BLOCK - 1 >= sj)).astype(jnp.float32)
    score = jnp.einsum('bgqk,kj->bgqj', imp, overlap)
    cur = (qpos // SLC_BLOCK)[:, None]
    j = jnp.arange(ns)[None, :]
    forced = (j == 0) | (j == cur) | (j == cur - 1)
    score = jnp.where(j > cur, -jnp.inf, jnp.where(forced, jnp.inf, score))
    n_sel = min(N_SELECT, ns)
    _, idx = lax.top_k(score, n_sel)
    bi = jnp.arange(B)[:, None, None, None]
    gi = jnp.arange(N_KV)[None, :, None, None]
    k_sel = ks[bi, gi, idx].reshape(B, N_KV, Q, n_sel * SLC_BLOCK, HEAD_DIM)
    v_sel = vs[bi, gi, idx].reshape(B, N_KV, Q, n_sel * SLC_BLOCK, HEAD_DIM)
    pos_sel = (idx[..., None] * SLC_BLOCK + jnp.arange(SLC_BLOCK)).reshape(B, N_KV, Q, n_sel * SLC_BLOCK)
    m_s = (pos_sel <= qpos[None, None, :, None])[:, :, None]
    s_s = jnp.einsum('bqghd,bgqkd->bghqk', q, k_sel).astype(jnp.float32) * ATTN_SCALE
    p_s = masked_softmax(s_s, m_s)
    o_s = jnp.einsum('bghqk,bgqkd->bqghd', p_s.astype(q.dtype), v_sel)
    dpos = qpos[:, None] - wpos[None, :]
    m_w = (dpos >= 0) & (dpos < WINDOW) & (wpos[None, :] >= 0)
    s_w = jnp.einsum('bqghd,bkgd->bghqk', q, kw).astype(jnp.float32) * ATTN_SCALE
    p_w = masked_softmax(s_w, m_w)
    o_w = jnp.einsum('bghqk,bkgd->bqghd', p_w.astype(q.dtype), vw)
    g = jax.nn.sigmoid(gates.astype(jnp.float32)).astype(q.dtype)
    return g[..., 0:1] * o_c + g[..., 1:2] * o_s + g[..., 2:3] * o_w


def nsa_prompt(q, kv_cmp, kv_slc, kv_win, gates, P):
    B, T = q.shape[0], q.shape[1]
    kc, vc = compress(kv_cmp, P['cmp_w1'], P['cmp_b1'], P['cmp_w2'], P['cmp_pe'])
    ks, vs = to_blocks(kv_slc)
    kwp = jnp.pad(kv_win, ((0, 0), (WINDOW, 0), (0, 0), (0, 0), (0, 0)))
    def one(i):
        s = i * Q_BLOCK
        qb = lax.dynamic_slice_in_dim(q, s, Q_BLOCK, axis=1)
        gb = lax.dynamic_slice_in_dim(gates, s, Q_BLOCK, axis=1)
        kwb = lax.dynamic_slice_in_dim(kwp, s, WINDOW + Q_BLOCK, axis=1)
        qpos = s + jnp.arange(Q_BLOCK, dtype=jnp.int32)
        wpos = s - WINDOW + jnp.arange(WINDOW + Q_BLOCK, dtype=jnp.int32)
        return nsa_core(qb, qpos, kc, vc, ks, vs, kwb[:, :, 0], kwb[:, :, 1], wpos, gb)
    o = lax.map(one, jnp.arange(T // Q_BLOCK, dtype=jnp.int32))
    o = jnp.moveaxis(o, 0, 1).reshape(B, T, ATTN_W)
    return o, kv_win[:, -min(WINDOW, T):]


def gather_pages(cache, page_table):
    pages = cache[page_table]
    return pages.reshape((page_table.shape[0], -1) + cache.shape[2:])


def nsa_sample(q, kv_cmp, kv_slc, kv_win, gates, P, cache_cmp, cache_slc, page_table, win_buf, past):
    DB, S = q.shape[0], q.shape[1]
    all_cmp = jnp.concatenate([gather_pages(cache_cmp, page_table).astype(kv_cmp.dtype), kv_cmp], axis=1)
    all_slc = jnp.concatenate([gather_pages(cache_slc, page_table).astype(kv_slc.dtype), kv_slc], axis=1)
    kc, vc = compress(all_cmp, P['cmp_w1'], P['cmp_b1'], P['cmp_w2'], P['cmp_pe'])
    ks, vs = to_blocks(all_slc)
    wb = win_buf.shape[1]
    kw = jnp.concatenate([win_buf.astype(kv_win.dtype), kv_win], axis=1)
    qpos = past + jnp.arange(S, dtype=jnp.int32)
    wpos = past - wb + jnp.arange(wb + S, dtype=jnp.int32)
    o = nsa_core(q, qpos, kc, vc, ks, vs, kw[:, :, 0], kw[:, :, 1], wpos, gates)
    return o.reshape(DB, S, ATTN_W), kw[:, -wb:]


def ffn_sublayer(x, mods, k, w_up, w_down, g, b):
    shift, scale, gate = mods[:, 3 * k], mods[:, 3 * k + 1], mods[:, 3 * k + 2]
    h = x * (1 + scale) + shift
    a, v = jnp.split(h @ w_up, 2, axis=-1)
    y = (jax.nn.silu(a) * v) @ w_down
    return layer_norm(ALPHA * x + 0.5 * gate * y, g, b)


def mixer_sublayer(x, mods, P, pos0, conv_buf, h0, pool_buf, attend):
    B, T, _ = x.shape
    h = x * (1 + mods[:, 4]) + mods[:, 3]
    sizes = [D_RNN, D_RNN, D_POOL, ATTN_W, KV_W, 3 * N_HEADS]
    cuts = [int(v) for v in np.cumsum(sizes)]
    u_rnn, g_rnn, u_pool, q, kv, g_nsa, g_mrg = jnp.split(h @ P['w_in'] + P['b_in'], cuts, axis=-1)
    y_rnn, conv_new, h_new = rglru(u_rnn, g_rnn, conv_buf, h0, P['conv_w'], P['conv_b'], P['gate_w'], P['gate_b'], P['lam'])
    y_pool, pool_new = pool_mix(u_pool, pool_buf, pos0, P['pool_w'], P['pool_scale'])
    pos = pos0 + jnp.arange(T, dtype=jnp.int32)
    q = rotary(q.reshape(B, T, N_KV, HEADS_PER_KV, HEAD_DIM), pos)
    kv = kv.reshape(B, T, 3, 2, N_KV, HEAD_DIM)
    kv_cmp = kv[:, :, 0]
    kv_slc = rope_kv(kv[:, :, 1], pos)
    kv_win = rope_kv(kv[:, :, 2], pos)
    y_attn, win_new = attend(q, kv_cmp, kv_slc, kv_win, g_nsa.reshape(B, T, N_KV, HEADS_PER_KV, 3))
    gm = jax.nn.sigmoid(g_mrg.astype(jnp.float32)).astype(x.dtype).reshape(B, T, 3, D_MODEL)
    merged = (gm[:, :, 0] * (y_rnn @ P['w_rnn_out']) + gm[:, :, 1] * (y_pool @ P['w_pool_out'])
              + gm[:, :, 2] * (y_attn @ P['w_attn_out']))
    out = merged @ P['w_out']
    x = layer_norm(ALPHA * x + mods[:, 5] * out, P['ln_g'], P['ln_b'])
    return x, (kv_cmp, kv_slc, win_new, h_new, conv_new, pool_new)


def setup_inputs(seed: int = 0) -> dict:
    key = jax.random.key(seed)
    keys = jax.random.split(key, 40)
    ctr = [0]
    def nk():
        ctr[0] += 1
        return keys[ctr[0] - 1]
    def nrm(shape, scale=1.0):
        return jax.random.normal(nk(), shape, jnp.float32) * scale
    n_pages = PAST_LEN // PAGE_SIZE
    n_used = DEC_BATCH * n_pages
    n_pool = n_used + n_used // 4
    win_buf = min(WINDOW, PAST_LEN)
    lam_a = jax.random.uniform(nk(), (DEPTH, D_RNN), jnp.float32, 0.9, 0.999)
    page_table = jax.random.permutation(nk(), n_pool)[:n_used].reshape(DEC_BATCH, n_pages).astype(jnp.int32)
    return {
        'x_prompt': nrm((BATCH, SEQ, D_MODEL)),
        'x_sample': nrm((DEC_BATCH, DEC_SEQ, D_MODEL)),
        'c_prompt': nrm((BATCH, D_MODEL)),
        'c_sample': nrm((DEC_BATCH, D_MODEL)),
        'cache_cmp_kv': nrm((DEPTH, n_pool, PAGE_SIZE, 2, N_KV, HEAD_DIM)),
        'cache_slc_kv': nrm((DEPTH, n_pool, PAGE_SIZE, 2, N_KV, HEAD_DIM)),
        'state_win_kv': nrm((DEPTH, DEC_BATCH, win_buf, 2, N_KV, HEAD_DIM)),
        'state_rnn_h': nrm((DEPTH, DEC_BATCH, D_RNN), 0.5),
        'state_rnn_conv': nrm((DEPTH, DEC_BATCH, CONV_W - 1, D_RNN)),
        'state_pool': nrm((DEPTH, DEC_BATCH, POOL_BUF, D_POOL)),
        'page_table': page_table,
        'ada_w': nrm((DEPTH, D_MODEL, 9 * D_MODEL), 0.5 * D_MODEL ** -0.5),
        'ada_b': nrm((DEPTH, 9 * D_MODEL), 0.02),
        'ln_g': 1.0 + nrm((DEPTH, 3, D_MODEL), 0.02),
        'ln_b': nrm((DEPTH, 3, D_MODEL), 0.02),
        'ffn_w_up': nrm((DEPTH, 2, D_MODEL, 2 * D_FF), D_MODEL ** -0.5),
        'ffn_w_down': nrm((DEPTH, 2, D_FF, D_MODEL), BETA * D_FF ** -0.5),
        'w_in': nrm((DEPTH, D_MODEL, IN_W), D_MODEL ** -0.5),
        'b_in': nrm((DEPTH, IN_W), 0.02),
        'rnn_conv_w': nrm((DEPTH, CONV_W, D_RNN), CONV_W ** -0.5),
        'rnn_conv_b': nrm((DEPTH, D_RNN), 0.02),
        'rnn_gate_w': nrm((DEPTH, 2, N_RNN_BLOCKS, RNN_BW, RNN_BW), RNN_BW ** -0.5),
        'rnn_gate_b': nrm((DEPTH, 2, D_RNN), 0.02),
        'rnn_lambda': jnp.log(lam_a) - jnp.log1p(-lam_a),
        'pool_w': nrm((DEPTH, len(POOL_WINDOWS), POOL_GW, POOL_GW), POOL_GW ** -0.5),
        'pool_scale': 1.0 + nrm((DEPTH, D_POOL), 0.1),
        'cmp_w1': nrm((DEPTH, 2, CMP_BLOCK * HEAD_DIM, CMP_HIDDEN), (CMP_BLOCK * HEAD_DIM) ** -0.5),
        'cmp_b1': nrm((DEPTH, 2, CMP_HIDDEN), 0.02),
        'cmp_w2': nrm((DEPTH, 2, CMP_HIDDEN, HEAD_DIM), CMP_HIDDEN ** -0.5),
        'cmp_pe': nrm((DEPTH, 2, CMP_BLOCK, HEAD_DIM), 0.5),
        'w_rnn_out': nrm((DEPTH, D_RNN, D_MODEL), BETA * D_RNN ** -0.5),
        'w_pool_out': nrm((DEPTH, D_POOL, D_MODEL), BETA * D_POOL ** -0.5),
        'w_attn_out': nrm((DEPTH, ATTN_W, D_MODEL), BETA * ATTN_W ** -0.5),
        'w_out': nrm((DEPTH, D_MODEL, D_MODEL), BETA * D_MODEL ** -0.5),
    }


def reference(x_prompt, x_sample, c_prompt, c_sample, cache_cmp_kv, cache_slc_kv, state_win_kv, state_rnn_h,
              state_rnn_conv, state_pool, page_table, ada_w, ada_b, ln_g, ln_b, ffn_w_up, ffn_w_down, w_in, b_in,
              rnn_conv_w, rnn_conv_b, rnn_gate_w, rnn_gate_b, rnn_lambda, pool_w, pool_scale, cmp_w1, cmp_b1,
              cmp_w2, cmp_pe, w_rnn_out, w_pool_out, w_attn_out, w_out):
    xp, xs = x_prompt, x_sample
    B = xp.shape[0]
    past = page_table.shape[1] * cache_cmp_kv.shape[2]
    conv0 = jnp.zeros((B, CONV_W - 1, D_RNN), xp.dtype)
    h00 = jnp.zeros((B, D_RNN), xp.dtype)
    pool0 = jnp.zeros((B, POOL_BUF, D_POOL), xp.dtype)
    cmp_p, cmp_s, slc_p, slc_s, win_p, win_s = [], [], [], [], [], []
    h_p, h_s, conv_p, conv_s, pool_p, pool_s = [], [], [], [], [], []
    for l in range(DEPTH):
        mp = adaln(c_prompt, ada_w[l], ada_b[l])
        ms = adaln(c_sample, ada_w[l], ada_b[l])
        xp = ffn_sublayer(xp, mp, 0, ffn_w_up[l, 0], ffn_w_down[l, 0], ln_g[l, 0], ln_b[l, 0])
        xs = ffn_sublayer(xs, ms, 0, ffn_w_up[l, 0], ffn_w_down[l, 0], ln_g[l, 0], ln_b[l, 0])
        P = {'w_in': w_in[l], 'b_in': b_in[l], 'conv_w': rnn_conv_w[l], 'conv_b': rnn_conv_b[l],
             'gate_w': rnn_gate_w[l], 'gate_b': rnn_gate_b[l], 'lam': rnn_lambda[l],
             'pool_w': pool_w[l], 'pool_scale': pool_scale[l],
             'cmp_w1': cmp_w1[l], 'cmp_b1': cmp_b1[l], 'cmp_w2': cmp_w2[l], 'cmp_pe': cmp_pe[l],
             'w_rnn_out': w_rnn_out[l], 'w_pool_out': w_pool_out[l], 'w_attn_out': w_attn_out[l],
             'w_out': w_out[l], 'ln_g': ln_g[l, 1], 'ln_b': ln_b[l, 1]}
        attend_p = functools.partial(nsa_prompt, P=P)
        attend_s = functools.partial(nsa_sample, P=P, cache_cmp=cache_cmp_kv[l], cache_slc=cache_slc_kv[l],
                                     page_table=page_table, win_buf=state_win_kv[l], past=past)
        xp, st_p = mixer_sublayer(xp, mp, P, 0, conv0, h00, pool0, attend_p)
        xs, st_s = mixer_sublayer(xs, ms, P, past, state_rnn_conv[l], state_rnn_h[l], state_pool[l], attend_s)
        xp = ffn_sublayer(xp, mp, 2, ffn_w_up[l, 1], ffn_w_down[l, 1], ln_g[l, 2], ln_b[l, 2])
        xs = ffn_sublayer(xs, ms, 2, ffn_w_up[l, 1], ffn_w_down[l, 1], ln_g[l, 2], ln_b[l, 2])
        cmp_p.append(st_p[0]); slc_p.append(st_p[1]); win_p.append(st_p[2])
        h_p.append(st_p[3]); conv_p.append(st_p[4]); pool_p.append(st_p[5])
        cmp_s.append(st_s[0]); slc_s.append(st_s[1]); win_s.append(st_s[2])
        h_s.append(st_s[3]); conv_s.append(st_s[4]); pool_s.append(st_s[5])
    return (xp, xs,
            jnp.stack(cmp_p), jnp.stack(cmp_s),
            jnp.stack(slc_p), jnp.stack(slc_s),
            jnp.stack(win_p), jnp.stack(win_s),
            jnp.stack(h_p), jnp.stack(h_s),
            jnp.stack(conv_p), jnp.stack(conv_s),
            jnp.stack(pool_p), jnp.stack(pool_s))
```

```python
import functools
import math

import jax
import jax.numpy as jnp
import numpy as np
from jax import lax
from jax.experimental import pallas as pl
from jax.experimental.pallas import tpu as pltpu

D_MODEL = 1024
N_HEADS = 16
N_KV = 4
HEADS_PER_KV = N_HEADS // N_KV
HEAD_DIM = 64
ROT_DIM = HEAD_DIM // 4
ROT_HALF = ROT_DIM // 2
ROPE_THETA = 500000.0
CMP_BLOCK = 32
CMP_STRIDE = 16
CMP_HIDDEN = 128
SLC_BLOCK = 64
N_SELECT = 8
WINDOW = 512
ATTN_SCALE = HEAD_DIM ** -0.5
D_RNN = D_MODEL
N_RNN_BLOCKS = 16
RNN_BW = D_RNN // N_RNN_BLOCKS
CONV_W = 4
RG_C = 8.0
D_POOL = D_MODEL
POOL_WINDOWS = (2, 4, 8, 16)
POOL_GW = D_POOL // len(POOL_WINDOWS)
POOL_BUF = max(POOL_WINDOWS) - 1
D_FF = 2816
ATTN_W = N_HEADS * HEAD_DIM
KV_W = 3 * 2 * N_KV * HEAD_DIM
LN_EPS = 1e-5

LANE = 128
KV_ROWS = 2 * N_KV * HEAD_DIM
K_ROWS = N_KV * HEAD_DIM
QKV_ROWS = ATTN_W + KV_W
PROJ_A_COLS = 3 * D_MODEL + 3 * D_MODEL + LANE
GATE_COL_BLOCK = (6 * D_MODEL) // LANE
NEG_BIG = -1e30
VMEM_LIMIT = 56 * 1024 * 1024

F32 = jnp.float32
BF16 = jnp.bfloat16


def _bf(x):
    return x.astype(BF16)


def _dot(a, b):
    return jnp.dot(a, b, preferred_element_type=F32)


def _dot_nt(a, b):
    return lax.dot_general(a, b, (((1,), (1,)), ((), ())), preferred_element_type=F32)


def _gelu(x):
    c = math.sqrt(2.0 / math.pi)
    return x * (0.5 * (1.0 + jnp.tanh(c * (x + 0.044715 * (x * x * x)))))


def _sigmoid(x):
    return 1.0 / (1.0 + jnp.exp(-x))


def _silu(x):
    return x * _sigmoid(x)


def _layer_norm(z, g, b):
    mu = jnp.mean(z, axis=-1, keepdims=True)
    zc = z - mu
    var = jnp.mean(zc * zc, axis=-1, keepdims=True)
    return zc * lax.rsqrt(var + LN_EPS) * g + b


def _params(sem):
    return pltpu.CompilerParams(dimension_semantics=sem, vmem_limit_bytes=VMEM_LIMIT)


def _mod_rows(m_ref, j, per_row, b_idx, reps):
    if per_row:
        m = m_ref[j]
        return jnp.concatenate([m] * reps, axis=0) if reps > 1 else m
    return m_ref[j, pl.ds(b_idx, 1), :]


def _adaln_body(c_ref, w_ref, b_ref, o_ref):
    sc = _silu(c_ref[...])
    o_ref[...] = _dot(_bf(sc), _bf(w_ref[...])) + b_ref[...]


def _adaln(c_all, ada_w, ada_b):
    depth = ada_w.shape[0]
    rows = c_all.shape[0]
    b4 = ada_b.reshape(depth, 9, 1, D_MODEL)
    return pl.pallas_call(
        _adaln_body,
        grid=(depth, 9),
        in_specs=[
            pl.BlockSpec((rows, D_MODEL), lambda l, k: (0, 0)),
            pl.BlockSpec((None, D_MODEL, D_MODEL), lambda l, k: (l, 0, k)),
            pl.BlockSpec((None, None, 1, D_MODEL), lambda l, k: (l, k, 0, 0)),
        ],
        out_specs=pl.BlockSpec((None, None, rows, D_MODEL), lambda l, k: (l, k, 0, 0)),
        out_shape=jax.ShapeDtypeStruct((depth, 9, rows, D_MODEL), F32),
        compiler_params=_params(("arbitrary", "arbitrary")),
        name="adaln",
    )(c_all, ada_w, b4)


def _ffn_body(x_ref, m_ref, wa_ref, wv_ref, wd_ref, g_ref, b_ref, o_ref, h_scr, acc_scr,
              *, per_row, tiles_per_b, reps, alpha):
    i = pl.program_id(0)
    j = pl.program_id(1)
    b_idx = i // tiles_per_b

    @pl.when(j == 0)
    def _():
        shift = _mod_rows(m_ref, 0, per_row, b_idx, reps)
        scale = _mod_rows(m_ref, 1, per_row, b_idx, reps)
        h_scr[...] = _bf(x_ref[...] * (1.0 + scale) + shift)
        acc_scr[...] = jnp.zeros_like(acc_scr)

    h = h_scr[...]
    a = _dot(h, _bf(wa_ref[...]))
    v = _dot(h, _bf(wv_ref[...]))
    act = _bf(_silu(a) * v)
    acc_scr[...] += _dot(act, _bf(wd_ref[...]))

    @pl.when(j == pl.num_programs(1) - 1)
    def _():
        gate = _mod_rows(m_ref, 2, per_row, b_idx, reps)
        z = alpha * x_ref[...] + 0.5 * gate * acc_scr[...]
        o_ref[...] = _layer_norm(z, g_ref[...], b_ref[...])


def _ffn(x, mods, l, k, w_up, w_down, ln_g, ln_b, ln_idx, *, per_row, tm, rows_per_b, mod_row_block,
         mod_rows, alpha):
    m_tok = x.shape[0]
    tf = 256
    n_f = D_FF // tf
    reps = tm // mod_rows if per_row else 1
    tiles_per_b = max(rows_per_b // tm, 1)
    body = functools.partial(_ffn_body, per_row=per_row, tiles_per_b=tiles_per_b, reps=reps, alpha=alpha)
    w_idx = 0 if k == 0 else 1
    return pl.pallas_call(
        body,
        grid=(m_tok // tm, n_f),
        in_specs=[
            pl.BlockSpec((tm, D_MODEL), lambda i, j: (i, 0)),
            pl.BlockSpec((None, 3, mod_rows, D_MODEL), lambda i, j: (l, k, mod_row_block, 0)),
            pl.BlockSpec((None, None, D_MODEL, tf), lambda i, j: (l, w_idx, 0, j)),
            pl.BlockSpec((None, None, D_MODEL, tf), lambda i, j: (l, w_idx, 0, n_f + j)),
            pl.BlockSpec((None, None, tf, D_MODEL), lambda i, j: (l, w_idx, j, 0)),
            pl.BlockSpec((None, None, 1, D_MODEL), lambda i, j: (l, ln_idx, 0, 0)),
            pl.BlockSpec((None, None, 1, D_MODEL), lambda i, j: (l, ln_idx, 0, 0)),
        ],
        out_specs=pl.BlockSpec((tm, D_MODEL), lambda i, j: (i, 0)),
        out_shape=jax.ShapeDtypeStruct((m_tok, D_MODEL), F32),
        scratch_shapes=[pltpu.VMEM((tm, D_MODEL), BF16), pltpu.VMEM((tm, D_MODEL), F32)],
        compiler_params=_params(("arbitrary", "arbitrary")),
        name="ffn",
    )(x, mods, w_up, w_up, w_down, ln_g, ln_b)


def _proj_a_body(x_ref, m_ref, w_ref, b_ref, o_ref, h_scr, *, per_row, tiles_per_b, reps):
    i = pl.program_id(0)
    j = pl.program_id(1)

    @pl.when(j == 0)
    def _():
        b_idx = i // tiles_per_b
        shift = _mod_rows(m_ref, 0, per_row, b_idx, reps)
        scale = _mod_rows(m_ref, 1, per_row, b_idx, reps)
        h_scr[...] = _bf(x_ref[...] * (1.0 + scale) + shift)

    o_ref[...] = _dot_nt(h_scr[...], _bf(w_ref[...])) + b_ref[...]


def _proj_a(x, mods, l, wt_a, bias_a, *, per_row, tm, rows_per_b, mod_row_block, mod_rows):
    m_tok = x.shape[0]
    tn = 896
    reps = tm // mod_rows if per_row else 1
    tiles_per_b = max(rows_per_b // tm, 1)
    body = functools.partial(_proj_a_body, per_row=per_row, tiles_per_b=tiles_per_b, reps=reps)
    return pl.pallas_call(
        body,
        grid=(m_tok // tm, PROJ_A_COLS // tn),
        in_specs=[
            pl.BlockSpec((tm, D_MODEL), lambda i, j: (i, 0)),
            pl.BlockSpec((None, 3, mod_rows, D_MODEL), lambda i, j: (l, 1, mod_row_block, 0)),
            pl.BlockSpec((None, tn, D_MODEL), lambda i, j: (l, j, 0)),
            pl.BlockSpec((None, 1, tn), lambda i, j: (l, 0, j)),
        ],
        out_specs=pl.BlockSpec((tm, tn), lambda i, j: (i, j)),
        out_shape=jax.ShapeDtypeStruct((m_tok, PROJ_A_COLS), F32),
        scratch_shapes=[pltpu.VMEM((tm, D_MODEL), BF16)],
        compiler_params=_params(("arbitrary", "arbitrary")),
        name="proj_a",
    )(x, mods, wt_a, bias_a)


def _rope_rows(res, cos, sin, row0):
    x1 = res[row0:row0 + ROT_HALF, :]
    x2 = res[row0 + ROT_HALF:row0 + ROT_DIM, :]
    return x1 * cos - x2 * sin, x2 * cos + x1 * sin


def _proj_b_body(x_ref, m_ref, w_ref, b_ref, cos_ref, sin_ref, o_ref, h_scr,
                 *, per_row, tiles_per_b, reps, n_sub, tf):
    i = pl.program_id(0)
    j = pl.program_id(1)

    @pl.when(j == 0)
    def _():
        b_idx = i // tiles_per_b
        shift = _mod_rows(m_ref, 0, per_row, b_idx, reps)
        scale = _mod_rows(m_ref, 1, per_row, b_idx, reps)
        h_scr[...] = _bf(x_ref[...] * (1.0 + scale) + shift)

    res = _dot_nt(_bf(w_ref[...]), h_scr[...]) + b_ref[...]
    sub = res.shape[1] // n_sub

    def store(rows, val):
        if n_sub == 1:
            o_ref[rows, :] = val
        else:
            for s in range(n_sub):
                o_ref[s, rows, :] = val[:, s * sub:(s + 1) * sub]

    store(slice(0, tf), res)

    @pl.when(j > 0)
    def _():
        cos = cos_ref[...]
        sin = sin_ref[...]
        for hh in range(N_KV):
            r0 = hh * HEAD_DIM
            n1, n2 = _rope_rows(res, cos, sin, r0)
            store(slice(r0, r0 + ROT_HALF), n1)
            store(slice(r0 + ROT_HALF, r0 + ROT_DIM), n2)


def _proj_b(x, mods, l, wt_full, bias_b, cos_t, sin_t, *, per_row, tm, rows_per_b, mod_row_block,
            mod_rows, n_sub):
    m_tok = x.shape[0]
    tf = KV_ROWS
    n_f = KV_W // tf
    row_block0 = (3 * D_MODEL + ATTN_W) // tf
    reps = tm // mod_rows if per_row else 1
    tiles_per_b = max(rows_per_b // tm, 1)
    body = functools.partial(_proj_b_body, per_row=per_row, tiles_per_b=tiles_per_b, reps=reps,
                             n_sub=n_sub, tf=tf)
    if n_sub == 1:
        n_b = m_tok // rows_per_b
        out_shape = jax.ShapeDtypeStruct((n_b, KV_W, rows_per_b), F32)
        out_spec = pl.BlockSpec((None, tf, tm), lambda i, j: (i // tiles_per_b, j, i % tiles_per_b))
        cs_spec = pl.BlockSpec((ROT_HALF, tm), lambda i, j: (0, i % tiles_per_b))
    else:
        sub = m_tok // n_sub
        out_shape = jax.ShapeDtypeStruct((n_sub, KV_W, sub), F32)
        out_spec = pl.BlockSpec((n_sub, tf, sub), lambda i, j: (0, j, 0))
        cs_spec = pl.BlockSpec((ROT_HALF, tm), lambda i, j: (0, 0))
    return pl.pallas_call(
        body,
        grid=(m_tok // tm, n_f),
        in_specs=[
            pl.BlockSpec((tm, D_MODEL), lambda i, j: (i, 0)),
            pl.BlockSpec((None, 3, mod_rows, D_MODEL), lambda i, j: (l, 1, mod_row_block, 0)),
            pl.BlockSpec((None, tf, D_MODEL), lambda i, j: (l, row_block0 + j, 0)),
            pl.BlockSpec((None, tf, 1), lambda i, j: (l, j, 0)),
            cs_spec,
            cs_spec,
        ],
        out_specs=out_spec,
        out_shape=out_shape,
        scratch_shapes=[pltpu.VMEM((tm, D_MODEL), BF16)],
        compiler_params=_params(("arbitrary", "arbitrary")),
        name="proj_b",
    )(x, mods, wt_full, bias_b, cos_t, sin_t)


def _proj_q_body(x_ref, m_ref, w_ref, b_ref, c_ref, slo_ref, shi_ref, o_ref, *, per_row, tiles_per_b, reps):
    i = pl.program_id(0)
    b_idx = i // tiles_per_b
    shift = _mod_rows(m_ref, 0, per_row, b_idx, reps)
    scale = _mod_rows(m_ref, 1, per_row, b_idx, reps)
    h = _bf(x_ref[...] * (1.0 + scale) + shift)
    q = _dot_nt(h, _bf(w_ref[...])) + b_ref[...]
    width = q.shape[1]
    n_rep = width // LANE
    cos = jnp.concatenate([c_ref[...]] * n_rep, axis=1)
    s_lo = jnp.concatenate([slo_ref[...]] * n_rep, axis=1)
    s_hi = jnp.concatenate([shi_ref[...]] * n_rep, axis=1)
    o_ref[...] = (q * cos + pltpu.roll(q, ROT_HALF, axis=1) * s_lo
                  + pltpu.roll(q, width - ROT_HALF, axis=1) * s_hi)


def _proj_q(x, mods, l, wt_full, bias_q, rope_tabs, *, per_row, tm, rows_per_b, mod_row_block, mod_rows):
    m_tok = x.shape[0]
    reps = tm // mod_rows if per_row else 1
    tiles_per_b = max(rows_per_b // tm, 1)
    row_block0 = (3 * D_MODEL) // ATTN_W
    body = functools.partial(_proj_q_body, per_row=per_row, tiles_per_b=tiles_per_b, reps=reps)
    if per_row:
        tab_spec = pl.BlockSpec((tm, LANE), lambda i: (0, 0))
    else:
        tab_spec = pl.BlockSpec((tm, LANE), lambda i: (i % tiles_per_b, 0))
    return pl.pallas_call(
        body,
        grid=(m_tok // tm,),
        in_specs=[
            pl.BlockSpec((tm, D_MODEL), lambda i: (i, 0)),
            pl.BlockSpec((None, 3, mod_rows, D_MODEL), lambda i: (l, 1, mod_row_block, 0)),
            pl.BlockSpec((None, ATTN_W, D_MODEL), lambda i: (l, row_block0, 0)),
            pl.BlockSpec((None, 1, ATTN_W), lambda i: (l, 0, 0)),
            tab_spec, tab_spec, tab_spec,
        ],
        out_specs=pl.BlockSpec((tm, ATTN_W), lambda i: (i, 0)),
        out_shape=jax.ShapeDtypeStruct((m_tok, ATTN_W), F32),
        compiler_params=_params(("arbitrary",)),
        name="proj_q",
    )(x, mods, wt_full, bias_q, *rope_tabs)


def _rglru_gates(xc, wg_ref, gb_ref, lam_ref):
    nblk = D_RNN // 256
    r_parts, i_parts = [], []
    for blk in range(nblk):
        xb = _bf(xc[:, blk * 256:(blk + 1) * 256])
        r_parts.append(_dot(xb, _bf(wg_ref[0, blk])))
        i_parts.append(_dot(xb, _bf(wg_ref[1, blk])))
    r = _sigmoid(jnp.concatenate(r_parts, axis=1) + gb_ref[0:1, :])
    ig = _sigmoid(jnp.concatenate(i_parts, axis=1) + gb_ref[1:2, :])
    lam = lam_ref[...]
    softplus_neg = jnp.maximum(-lam, 0.0) + jnp.log(1.0 + jnp.exp(-jnp.abs(lam)))
    log_a = -RG_C * r * softplus_neg
    a = jnp.exp(log_a)
    b = jnp.sqrt(1.0 - a * a) * (ig * xc)
    return a, b


def _rglru_prompt_body(u_ref, g_ref, cw_ref, cb_ref, wg_ref, gb_ref, lam_ref,
                       y_ref, hl_ref, ul_ref, h_scr, c_scr, *, tt):
    t = pl.program_id(1)

    @pl.when(t == 0)
    def _():
        h_scr[...] = jnp.zeros_like(h_scr)
        c_scr[...] = jnp.zeros_like(c_scr)

    u = u_ref[...]
    ext = jnp.concatenate([c_scr[...], u], axis=0)
    xc = cb_ref[...] + u * cw_ref[CONV_W - 1:CONV_W, :]
    for back in range(1, CONV_W):
        sh = pltpu.roll(ext, back, axis=0)[8:8 + tt]
        xc = xc + sh * cw_ref[CONV_W - 1 - back:CONV_W - back, :]
    a, b = _rglru_gates(xc, wg_ref, gb_ref, lam_ref)

    row = lax.broadcasted_iota(jnp.int32, (tt, 1), 0)
    d = 1
    while d < tt:
        a_sh = pltpu.roll(a, d, axis=0)
        b_sh = pltpu.roll(b, d, axis=0)
        valid = row >= d
        b = jnp.where(valid, a * b_sh + b, b)
        a = jnp.where(valid, a * a_sh, a)
        d *= 2
    h = b + a * h_scr[0:1, :]
    y_ref[...] = h * _gelu(g_ref[...])
    h_scr[...] = jnp.broadcast_to(h[tt - 1:tt, :], h_scr.shape)
    c_scr[...] = u[tt - 8:tt, :]

    @pl.when(t == pl.num_programs(1) - 1)
    def _():
        hl_ref[...] = h[tt - 1:tt, :]
        ul_ref[...] = u[tt - 8:tt, :]


def _rglru_prompt(proj_a, l, n_b, seq, conv_w, conv_b, wg, gate_b, lam):
    tt = 256
    nt = seq // tt
    body = functools.partial(_rglru_prompt_body, tt=tt)
    return pl.pallas_call(
        body,
        grid=(n_b, nt),
        in_specs=[
            pl.BlockSpec((tt, D_RNN), lambda b, t: (b * nt + t, 0)),
            pl.BlockSpec((tt, D_RNN), lambda b, t: (b * nt + t, 1)),
            pl.BlockSpec((None, CONV_W, D_RNN), lambda b, t: (l, 0, 0)),
            pl.BlockSpec((None, 1, D_RNN), lambda b, t: (l, 0, 0)),
            pl.BlockSpec((None, 2, D_RNN // 256, 256, 256), lambda b, t: (l, 0, 0, 0, 0)),
            pl.BlockSpec((None, 2, D_RNN), lambda b, t: (l, 0, 0)),
            pl.BlockSpec((None, 1, D_RNN), lambda b, t: (l, 0, 0)),
        ],
        out_specs=[
            pl.BlockSpec((tt, D_RNN), lambda b, t: (b * nt + t, 0)),
            pl.BlockSpec((None, 1, D_RNN), lambda b, t: (b, 0, 0)),
            pl.BlockSpec((None, 8, D_RNN), lambda b, t: (b, 0, 0)),
        ],
        out_shape=[
            jax.ShapeDtypeStruct((n_b * seq, D_RNN), F32),
            jax.ShapeDtypeStruct((n_b, 1, D_RNN), F32),
            jax.ShapeDtypeStruct((n_b, 8, D_RNN), F32),
        ],
        scratch_shapes=[pltpu.VMEM((8, D_RNN), F32), pltpu.VMEM((8, D_RNN), F32)],
        compiler_params=_params(("arbitrary", "arbitrary")),
        name="rglru_prompt",
    )(proj_a, proj_a, conv_w, conv_b, wg, gate_b, lam)


def _rglru_sample_body(u_ref, g_ref, buf_ref, h0_ref, cw_ref, cb_ref, wg_ref, gb_ref, lam_ref,
                       y_ref, hn_ref, cn_ref, *, n_b, n_s):
    u = u_ref[...]
    slabs = [buf_ref[r] for r in range(CONV_W - 1)] + [u[s * n_b:(s + 1) * n_b] for s in range(n_s)]
    xcs = []
    for s in range(n_s):
        xc = cb_ref[...] + slabs[s] * cw_ref[0:1, :]
        for k in range(1, CONV_W):
            xc = xc + slabs[s + k] * cw_ref[k:k + 1, :]
        xcs.append(xc)
    xc = jnp.concatenate(xcs, axis=0)
    a, b = _rglru_gates(xc, wg_ref, gb_ref, lam_ref)
    h = h0_ref[...]
    hs = []
    for s in range(n_s):
        h = a[s * n_b:(s + 1) * n_b] * h + b[s * n_b:(s + 1) * n_b]
        hs.append(h)
    y_ref[...] = jnp.concatenate(hs, axis=0) * _gelu(g_ref[...])
    hn_ref[...] = h
    for r in range(CONV_W - 1):
        cn_ref[r] = slabs[n_s + r]


def _rglru_sample(proj_a, l, n_b, n_s, conv_buf, h0, conv_w, conv_b, wg, gate_b, lam):
    m_tok = n_b * n_s
    body = functools.partial(_rglru_sample_body, n_b=n_b, n_s=n_s)
    return pl.pallas_call(
        body,
        grid=(1,),
        in_specs=[
            pl.BlockSpec((m_tok, D_RNN), lambda i: (0, 0)),
            pl.BlockSpec((m_tok, D_RNN), lambda i: (0, 1)),
            pl.BlockSpec((None, CONV_W - 1, n_b, D_RNN), lambda i: (l, 0, 0, 0)),
            pl.BlockSpec((None, n_b, D_RNN), lambda i: (l, 0, 0)),
            pl.BlockSpec((None, CONV_W, D_RNN), lambda i: (l, 0, 0)),
            pl.BlockSpec((None, 1, D_RNN), lambda i: (l, 0, 0)),
            pl.BlockSpec((None, 2, D_RNN // 256, 256, 256), lambda i: (l, 0, 0, 0, 0)),
            pl.BlockSpec((None, 2, D_RNN), lambda i: (l, 0, 0)),
            pl.BlockSpec((None, 1, D_RNN), lambda i: (l, 0, 0)),
        ],
        out_specs=[
            pl.BlockSpec((m_tok, D_RNN), lambda i: (0, 0)),
            pl.BlockSpec((n_b, D_RNN), lambda i: (0, 0)),
            pl.BlockSpec((CONV_W - 1, n_b, D_RNN), lambda i: (0, 0, 0)),
        ],
        out_shape=[
            jax.ShapeDtypeStruct((m_tok, D_RNN), F32),
            jax.ShapeDtypeStruct((n_b, D_RNN), F32),
            jax.ShapeDtypeStruct((CONV_W - 1, n_b, D_RNN), F32),
        ],
        compiler_params=_params(("arbitrary",)),
        name="rglru_sample",
    )(proj_a, proj_a, conv_buf, h0, conv_w, conv_b, wg, gate_b, lam)


def _pool_prompt_body(u_ref, w_ref, sc_ref, y_ref, last_ref, halo_scr, *, tt):
    t = pl.program_id(1)

    @pl.when(t == 0)
    def _():
        halo_scr[...] = jnp.zeros_like(halo_scr)

    x = u_ref[...]
    ext = jnp.concatenate([halo_scr[...], x], axis=0)
    pos = t * tt + lax.broadcasted_iota(jnp.int32, (tt, 1), 0)
    outs = []
    for gi, w in enumerate(POOL_WINDOWS):
        cols = slice(gi * POOL_GW, (gi + 1) * POOL_GW)
        s = ext[:, cols]
        k = 1
        while k < w:
            s = s + pltpu.roll(s, k, axis=0)
            k *= 2
        cnt = jnp.minimum(pos + 1, w).astype(F32)
        d = s[16:16 + tt] / cnt - x[:, cols]
        outs.append(_dot(_bf(d), _bf(w_ref[gi])))
    y_ref[...] = jnp.concatenate(outs, axis=1) * sc_ref[...]
    halo_scr[...] = x[tt - 16:tt, :]

    @pl.when(t == pl.num_programs(1) - 1)
    def _():
        last_ref[...] = x[tt - 16:tt, :]


def _pool_prompt(proj_a, l, n_b, seq, pool_w, pool_scale):
    tt = 256
    nt = seq // tt
    body = functools.partial(_pool_prompt_body, tt=tt)
    return pl.pallas_call(
        body,
        grid=(n_b, nt),
        in_specs=[
            pl.BlockSpec((tt, D_POOL), lambda b, t: (b * nt + t, 2)),
            pl.BlockSpec((None, len(POOL_WINDOWS), POOL_GW, POOL_GW), lambda b, t: (l, 0, 0, 0)),
            pl.BlockSpec((None, 1, D_POOL), lambda b, t: (l, 0, 0)),
        ],
        out_specs=[
            pl.BlockSpec((tt, D_POOL), lambda b, t: (b * nt + t, 0)),
            pl.BlockSpec((None, 16, D_POOL), lambda b, t: (b, 0, 0)),
        ],
        out_shape=[
            jax.ShapeDtypeStruct((n_b * seq, D_POOL), F32),
            jax.ShapeDtypeStruct((n_b, 16, D_POOL), F32),
        ],
        scratch_shapes=[pltpu.VMEM((16, D_POOL), F32)],
        compiler_params=_params(("arbitrary", "arbitrary")),
        name="pool_prompt",
    )(proj_a, pool_w, pool_scale)


def _pool_sample_body(u_ref, buf_ref, w_ref, sc_ref, y_ref, pn_ref, *, n_b, n_s):
    u = u_ref[...]
    slabs = [buf_ref[r] for r in range(POOL_BUF)] + [u[s * n_b:(s + 1) * n_b] for s in range(n_s)]
    outs = []
    for gi, w in enumerate(POOL_WINDOWS):
        cols = slice(gi * POOL_GW, (gi + 1) * POOL_GW)
        ds = []
        for s in range(n_s):
            acc = slabs[POOL_BUF + s][:, cols]
            for back in range(1, w):
                acc = acc + slabs[POOL_BUF + s - back][:, cols]
            ds.append(acc / float(w) - slabs[POOL_BUF + s][:, cols])
        outs.append(_dot(_bf(jnp.concatenate(ds, axis=0)), _bf(w_ref[gi])))
    y_ref[...] = jnp.concatenate(outs, axis=1) * sc_ref[...]
    for r in range(POOL_BUF):
        pn_ref[r] = slabs[n_s + r]


def _pool_sample(proj_a, l, n_b, n_s, pool_buf, pool_w, pool_scale):
    m_tok = n_b * n_s
    body = functools.partial(_pool_sample_body, n_b=n_b, n_s=n_s)
    return pl.pallas_call(
        body,
        grid=(1,),
        in_specs=[
            pl.BlockSpec((m_tok, D_POOL), lambda i: (0, 2)),
            pl.BlockSpec((None, POOL_BUF, n_b, D_POOL), lambda i: (l, 0, 0, 0)),
            pl.BlockSpec((None, len(POOL_WINDOWS), POOL_GW, POOL_GW), lambda i: (l, 0, 0, 0)),
            pl.BlockSpec((None, 1, D_POOL), lambda i: (l, 0, 0)),
        ],
        out_specs=[
            pl.BlockSpec((m_tok, D_POOL), lambda i: (0, 0)),
            pl.BlockSpec((POOL_BUF, n_b, D_POOL), lambda i: (0, 0, 0)),
        ],
        out_shape=[
            jax.ShapeDtypeStruct((m_tok, D_POOL), F32),
            jax.ShapeDtypeStruct((POOL_BUF, n_b, D_POOL), F32),
        ],
        compiler_params=_params(("arbitrary",)),
        name="pool_sample",
    )(proj_a, pool_buf, pool_w, pool_scale)


def _pe_term_body(pe_ref, w1_ref, b1_ref, o_ref):
    pe = jnp.broadcast_to(pe_ref[...], (8, pe_ref.shape[1]))
    o_ref[...] = _dot(_bf(pe), _bf(w1_ref[...]))[0:1, :] + b1_ref[...]


def _pe_term(cmp_pe, cmp_w1, cmp_b1):
    depth = cmp_pe.shape[0]
    flat = CMP_BLOCK * HEAD_DIM
    pe4 = cmp_pe.reshape(depth, 2, 1, flat)
    b4 = cmp_b1.reshape(depth, 2, 1, CMP_HIDDEN)
    return pl.pallas_call(
        _pe_term_body,
        grid=(depth, 2),
        in_specs=[
            pl.BlockSpec((None, None, 1, flat), lambda l, c: (l, c, 0, 0)),
            pl.BlockSpec((None, None, flat, CMP_HIDDEN), lambda l, c: (l, c, 0, 0)),
            pl.BlockSpec((None, None, 1, CMP_HIDDEN), lambda l, c: (l, c, 0, 0)),
        ],
        out_specs=pl.BlockSpec((None, None, 1, CMP_HIDDEN), lambda l, c: (l, c, 0, 0)),
        out_shape=jax.ShapeDtypeStruct((depth, 2, 1, CMP_HIDDEN), F32),
        compiler_params=_params(("arbitrary", "arbitrary")),
        name="pe_term",
    )(pe4, cmp_w1, b4)


def _compress_body(*refs, n_pages, page):
    pt_ref = refs[0]
    page_refs = refs[1:1 + n_pages]
    w1_ref, pet_ref, w2t_ref, o_ref, s_scr, x_scr = refs[1 + n_pages:]
    del pt_ref
    n_chunks = n_pages * page // CMP_STRIDE
    n_slab = KV_ROWS // LANE
    for p in range(n_pages):
        for j in range(n_slab):
            blk = page_refs[p][j * LANE:(j + 1) * LANE, :]
            s_scr[j, p * page:(p + 1) * page, :] = blk.T
    lane = lax.broadcasted_iota(jnp.int32, (n_chunks, LANE), 1)
    low = lane < HEAD_DIM
    for c in range(2):
        for gp in range(2):
            j = 2 * c + gp
            for r2 in range(CMP_STRIDE // 2):
                v0 = s_scr[j, pl.ds(2 * r2, n_chunks, stride=CMP_STRIDE), :]
                v1 = s_scr[j, pl.ds(2 * r2 + 1, n_chunks, stride=CMP_STRIDE), :]
                xe = jnp.where(low, v0, pltpu.roll(v1, HEAD_DIM, axis=1))
                xo = jnp.where(low, pltpu.roll(v0, HEAD_DIM, axis=1), v1)
                x_scr[(2 * gp) * n_chunks:(2 * gp + 1) * n_chunks, r2 * LANE:(r2 + 1) * LANE] = _bf(xe)
                x_scr[(2 * gp + 1) * n_chunks:(2 * gp + 2) * n_chunks, r2 * LANE:(r2 + 1) * LANE] = _bf(xo)
        pm = _dot(x_scr[...], _bf(w1_ref[c]))
        rows = pm.shape[0]
        pre = pm[:, :CMP_HIDDEN] + pltpu.roll(pm[:, CMP_HIDDEN:], rows - 1, axis=0)
        hid = _gelu(pre + pet_ref[c])
        out_t = _dot_nt(_bf(w2t_ref[c]), _bf(hid))
        for g in range(N_KV):
            o_ref[c, g * HEAD_DIM:(g + 1) * HEAD_DIM, :] = out_t[:, g * n_chunks:(g + 1) * n_chunks]


def _compress(page_table, src, page_specs, n_seq, l, w1c, pe_term, w2t):
    n_pages = len(page_specs)
    page = LANE
    n_chunks = n_pages * page // CMP_STRIDE
    body = functools.partial(_compress_body, n_pages=n_pages, page=page)
    grid_spec = pltpu.PrefetchScalarGridSpec(
        num_scalar_prefetch=1,
        grid=(n_seq,),
        in_specs=list(page_specs) + [
            pl.BlockSpec((None, 2, CMP_STRIDE * HEAD_DIM, 2 * CMP_HIDDEN), lambda b, pt: (l, 0, 0, 0)),
            pl.BlockSpec((None, 2, 1, CMP_HIDDEN), lambda b, pt: (l, 0, 0, 0)),
            pl.BlockSpec((None, 2, HEAD_DIM, CMP_HIDDEN), lambda b, pt: (l, 0, 0, 0)),
        ],
        out_specs=pl.BlockSpec((None, 2, K_ROWS, n_chunks), lambda b, pt: (b, 0, 0, 0)),
        scratch_shapes=[
            pltpu.VMEM((KV_ROWS // LANE, n_pages * page, LANE), F32),
            pltpu.VMEM((N_KV * n_chunks, CMP_STRIDE * HEAD_DIM), BF16),
        ],
    )
    return pl.pallas_call(
        body,
        grid_spec=grid_spec,
        out_shape=jax.ShapeDtypeStruct((n_seq, 2, K_ROWS, n_chunks), F32),
        compiler_params=_params(("arbitrary",)),
        name="compress",
    )(page_table, *([src] * n_pages), w1c, pe_term, w2t)


def _select_blocks(score_t, cur, n_blocks):
    rows = score_t.shape[0]
    j = lax.broadcasted_iota(jnp.int32, score_t.shape, 0)
    forced = (j == 0) | (j == cur) | (j == cur - 1)
    sc = jnp.where(j > cur, -jnp.inf, jnp.where(forced, jnp.inf, score_t))
    rank = jnp.zeros(score_t.shape, jnp.int32)
    for jp in range(n_blocks):
        row = sc[jp:jp + 1, :]
        ahead = (row > sc) | ((row == sc) & (jp < j))
        rank = rank + ahead.astype(jnp.int32)
    del rows
    return ((rank < N_SELECT) & (j <= cur)).astype(F32)


def _softmax_update(s, m, l_sum, acc, v_t):
    m_new = jnp.maximum(m, jnp.max(s, axis=-1, keepdims=True))
    alpha = jnp.exp(m - m_new)
    p = jnp.exp(s - m_new)
    l_new = alpha * l_sum + jnp.sum(p, axis=-1, keepdims=True)
    acc_new = alpha * acc + _dot_nt(_bf(p), v_t)
    return m_new, l_new, acc_new


def _stack_heads(k_t):
    return _bf(jnp.concatenate([k_t] * HEADS_PER_KV, axis=0))


def _attn_prompt_body(q_ref, sk_ref, sv_ref, wk_ref, wv_ref, kc_ref, vc_ref, gate_ref, e_ref,
                      o_ref, imp_scr, *, tq, kc, seq):
    g = pl.program_id(1)
    i = pl.program_id(2)
    q0 = i * tq
    hp = HEADS_PER_KV
    rows = hp * tq
    q_w = hp * HEAD_DIM
    q_blk = q_ref[...] * ATTN_SCALE
    lane_head = lax.broadcasted_iota(jnp.int32, (1, q_w), 1) // HEAD_DIM
    q_all = _bf(jnp.concatenate([jnp.where(lane_head == h, q_blk, 0.0) for h in range(hp)], axis=0))
    qpos = q0 + lax.broadcasted_iota(jnp.int32, (rows, 1), 0) % tq

    n_blk = kc_ref.shape[1]
    blk_end = lax.broadcasted_iota(jnp.int32, (1, n_blk), 1) * CMP_STRIDE + (CMP_BLOCK - 1)
    cmask = blk_end <= qpos
    s_c = jnp.where(cmask, _dot(q_all, _stack_heads(kc_ref[...])), NEG_BIG)
    m_c = jnp.max(s_c, axis=-1, keepdims=True)
    e_c = jnp.where(cmask, jnp.exp(s_c - m_c), 0.0)
    p_c = e_c / jnp.maximum(jnp.sum(e_c, axis=-1, keepdims=True), 1e-30)
    o_c = _dot_nt(_bf(p_c), _bf(vc_ref[...]))

    imp = p_c[0:tq]
    for h in range(1, hp):
        imp = imp + p_c[h * tq:(h + 1) * tq]
    imp_scr[0:8, :] = jnp.zeros((8, tq), F32)
    imp_scr[8:8 + n_blk, :] = imp.T
    n_sel_blocks = seq // SLC_BLOCK
    ratio = SLC_BLOCK // CMP_STRIDE
    score_t = imp_scr[pl.ds(7, n_sel_blocks, stride=ratio), :]
    for o in range(1, ratio + 1):
        score_t = score_t + imp_scr[pl.ds(7 + o, n_sel_blocks, stride=ratio), :]
    cur = (q0 + lax.broadcasted_iota(jnp.int32, (1, tq), 1)) // SLC_BLOCK
    sel_t = _select_blocks(score_t, cur, n_sel_blocks)
    sel_pad = jnp.concatenate([sel_t, jnp.zeros((LANE - n_sel_blocks, tq), F32)], axis=0)
    sel_q = _bf(sel_pad.T)

    def sel_step(c, carry):
        m, l_sum, acc = carry
        k0 = pl.multiple_of(c * kc, kc)
        picked = _dot(sel_q, e_ref[:, pl.ds(k0, kc)])
        key = k0 + lax.broadcasted_iota(jnp.int32, (1, kc), 1)
        picked = jnp.concatenate([picked] * hp, axis=0)
        ok = (picked > 0.5) & (key <= qpos)
        s = jnp.where(ok, _dot(q_all, _stack_heads(sk_ref[:, pl.ds(k0, kc)])), NEG_BIG)
        return _softmax_update(s, m, l_sum, acc, _bf(sv_ref[:, pl.ds(k0, kc)]))

    init = (jnp.full((rows, 1), NEG_BIG, F32), jnp.zeros((rows, 1), F32), jnp.zeros((rows, HEAD_DIM), F32))
    n_sel_chunks = (q0 + tq + kc - 1) // kc
    _, l_s, acc_s = lax.fori_loop(0, n_sel_chunks, sel_step, init)
    o_s = acc_s / l_s

    def win_step(c, carry):
        m, l_sum, acc = carry
        k0 = pl.multiple_of(c * kc, kc)
        key = k0 + lax.broadcasted_iota(jnp.int32, (1, kc), 1)
        dpos = qpos - key
        ok = (dpos >= 0) & (dpos < WINDOW)
        s = jnp.where(ok, _dot(q_all, _stack_heads(wk_ref[:, pl.ds(k0, kc)])), NEG_BIG)
        return _softmax_update(s, m, l_sum, acc, _bf(wv_ref[:, pl.ds(k0, kc)]))

    c_lo = jnp.maximum(q0 - WINDOW + 1, 0) // kc
    _, l_w, acc_w = lax.fori_loop(c_lo, n_sel_chunks, win_step, init)
    o_w = acc_w / l_w

    gates = _sigmoid(gate_ref[...])
    d_i = lax.broadcasted_iota(jnp.int32, (HEAD_DIM, q_w), 0)
    l_i = lax.broadcasted_iota(jnp.int32, (HEAD_DIM, q_w), 1)
    out = jnp.zeros((tq, q_w), F32)
    for h in range(hp):
        rs = slice(h * tq, (h + 1) * tq)
        acc = jnp.zeros((tq, HEAD_DIM), F32)
        for br, o_b in enumerate((o_c, o_s, o_w)):
            gsel = jnp.zeros((tq, 1), F32)
            for gg in range(N_KV):
                cidx = (gg * hp + h) * 3 + br
                gsel = gsel + jnp.where(g == gg, gates[:, cidx:cidx + 1], 0.0)
            acc = acc + gsel * o_b[rs]
        place = (l_i == d_i + h * HEAD_DIM).astype(BF16)
        out = out + _dot(_bf(acc), place)
    o_ref[...] = out


def _attn_prompt(q_tok, kv_t, c_t, proj_a, e_mat, n_b, seq):
    tq = 128
    kc = 256
    nq = seq // tq
    q_rows = HEADS_PER_KV * HEAD_DIM
    slc0 = KV_ROWS // HEAD_DIM
    win0 = 2 * KV_ROWS // HEAD_DIM
    body = functools.partial(_attn_prompt_body, tq=tq, kc=kc, seq=seq)
    n_blk = c_t.shape[-1]
    return pl.pallas_call(
        body,
        grid=(n_b, N_KV, nq),
        in_specs=[
            pl.BlockSpec((tq, q_rows), lambda b, g, i: (b * nq + i, g)),
            pl.BlockSpec((None, HEAD_DIM, seq), lambda b, g, i: (b, slc0 + g, 0)),
            pl.BlockSpec((None, HEAD_DIM, seq), lambda b, g, i: (b, slc0 + N_KV + g, 0)),
            pl.BlockSpec((None, HEAD_DIM, seq), lambda b, g, i: (b, win0 + g, 0)),
            pl.BlockSpec((None, HEAD_DIM, seq), lambda b, g, i: (b, win0 + N_KV + g, 0)),
            pl.BlockSpec((None, None, HEAD_DIM, n_blk), lambda b, g, i: (b, 0, g, 0)),
            pl.BlockSpec((None, None, HEAD_DIM, n_blk), lambda b, g, i: (b, 1, g, 0)),
            pl.BlockSpec((tq, LANE), lambda b, g, i: (b * nq + i, GATE_COL_BLOCK)),
            pl.BlockSpec((LANE, seq), lambda b, g, i: (0, 0)),
        ],
        out_specs=pl.BlockSpec((tq, q_rows), lambda b, g, i: (b * nq + i, g)),
        out_shape=jax.ShapeDtypeStruct((n_b * seq, ATTN_W), F32),
        scratch_shapes=[pltpu.VMEM((8 + n_blk + 8, tq), F32)],
        compiler_params=_params(("arbitrary", "arbitrary", "arbitrary")),
        name="attn_prompt",
    )(q_tok, kv_t, kv_t, kv_t, kv_t, c_t, c_t, proj_a, e_mat)


def _attn_sample_body(*refs, n_pages, page, n_s, past):
    pt_ref = refs[0]
    page_refs = refs[1:1 + n_pages]
    q_ref, win_ref, new_ref, c_ref, gate_ref, e_ref, o_ref, s_scr, imp_scr = refs[1 + n_pages:]
    del pt_ref
    hp = HEADS_PER_KV
    rpg = n_s * hp
    rows = N_KV * rpg
    n_keys = n_pages * page
    row_i = lax.broadcasted_iota(jnp.int32, (rows, 1), 0)
    row_g = row_i // rpg
    row_s = (row_i % rpg) // hp
    col_g = lax.broadcasted_iota(jnp.int32, (1, K_ROWS), 1) // HEAD_DIM
    q2 = q_ref[...] * ATTN_SCALE
    diag = row_g == col_g
    q_bd = _bf(jnp.where(diag, jnp.concatenate([q2] * (K_ROWS // LANE), axis=1), 0.0))

    def fold(o_full):
        z = jnp.where(diag, o_full, 0.0)
        w = z[:, 0:LANE]
        for blk in range(1, K_ROWS // LANE):
            w = w + z[:, blk * LANE:(blk + 1) * LANE]
        return w + pltpu.roll(w, HEAD_DIM, axis=1)

    n_blk = c_ref.shape[-1]
    blk_end = lax.broadcasted_iota(jnp.int32, (1, n_blk), 1) * CMP_STRIDE + (CMP_BLOCK - 1)
    cmask = blk_end <= past + row_s
    s_c = jnp.where(cmask, _dot(q_bd, _bf(c_ref[0])), NEG_BIG)
    m_c = jnp.max(s_c, axis=-1, keepdims=True)
    e_c = jnp.where(cmask, jnp.exp(s_c - m_c), 0.0)
    p_c = e_c / jnp.maximum(jnp.sum(e_c, axis=-1, keepdims=True), 1e-30)
    o_c = fold(_dot_nt(_bf(p_c), _bf(c_ref[1])))

    t1 = p_c + pltpu.roll(p_c, rows - 1, axis=0)
    imp = t1 + pltpu.roll(t1, rows - 2, axis=0)
    imp_pad = jnp.concatenate([imp, jnp.zeros((LANE - rows, n_blk), F32)], axis=0)
    imp_scr[...] = jnp.zeros(imp_scr.shape, F32)
    imp_scr[8:8 + n_blk, :] = imp_pad.T
    n_sel_rows = 40
    n_sel_blocks = (past + n_s + SLC_BLOCK - 1) // SLC_BLOCK
    ratio = SLC_BLOCK // CMP_STRIDE
    score_t = imp_scr[pl.ds(7, n_sel_rows, stride=ratio), :]
    for o in range(1, ratio + 1):
        score_t = score_t + imp_scr[pl.ds(7 + o, n_sel_rows, stride=ratio), :]
    lane_row = lax.broadcasted_iota(jnp.int32, (1, LANE), 1)
    cur = (past + (lane_row % rpg) // hp) // SLC_BLOCK
    sel_t = _select_blocks(score_t, cur, n_sel_blocks)
    sel_pad = jnp.concatenate([sel_t, jnp.zeros((LANE - n_sel_rows, LANE), F32)], axis=0)
    sel_rows = _bf(sel_pad.T)
    expand = (lax.broadcasted_iota(jnp.int32, (rows, LANE), 1) == (row_i // hp) * hp).astype(BF16)
    sel64 = _bf(_dot(expand, sel_rows))

    n_new = 16
    new_all = new_ref[...]
    new_sk = _bf(new_all[:, KV_ROWS:KV_ROWS + K_ROWS])
    new_sv = _bf(new_all[:, KV_ROWS + K_ROWS:2 * KV_ROWS])
    new_wk = _bf(new_all[:, 2 * KV_ROWS:2 * KV_ROWS + K_ROWS])
    new_wv = _bf(new_all[:, 2 * KV_ROWS + K_ROWS:3 * KV_ROWS])
    new_i = lax.broadcasted_iota(jnp.int32, (1, n_new), 1)
    new_ok = (new_i <= row_s) & (new_i < n_s)

    for p in range(n_pages):
        s_scr[:, p * page:(p + 1) * page] = _dot(q_bd, _bf(page_refs[p][0:K_ROWS, :]))
    picked = _dot(sel64, e_ref[...])
    s_all = jnp.where(picked > 0.5, s_scr[...], NEG_BIG)
    s_new = jnp.where(new_ok, _dot_nt(q_bd, new_sk), NEG_BIG)
    m_s = jnp.maximum(jnp.max(s_all, axis=-1, keepdims=True), jnp.max(s_new, axis=-1, keepdims=True))
    p_all = jnp.exp(s_all - m_s)
    p_new = jnp.exp(s_new - m_s)
    l_s = jnp.sum(p_all, axis=-1, keepdims=True) + jnp.sum(p_new, axis=-1, keepdims=True)
    acc = _dot(_bf(p_new), new_sv)
    p_all = _bf(p_all)
    for p in range(n_pages):
        acc = acc + _dot_nt(p_all[:, p * page:(p + 1) * page], _bf(page_refs[p][K_ROWS:KV_ROWS, :]))
    o_s = fold(acc) / l_s

    wlen = win_ref.shape[1]
    kk = lax.broadcasted_iota(jnp.int32, (1, wlen), 1)
    dpos = (past + row_s) - (past - wlen + kk)
    w_ok = (dpos >= 0) & (dpos < WINDOW)
    s_w = jnp.where(w_ok, _dot(q_bd, _bf(win_ref[0:K_ROWS, :])), NEG_BIG)
    s_wn = jnp.where(new_ok, _dot_nt(q_bd, new_wk), NEG_BIG)
    m_w = jnp.maximum(jnp.max(s_w, axis=-1, keepdims=True), jnp.max(s_wn, axis=-1, keepdims=True))
    p_w = jnp.exp(s_w - m_w)
    p_wn = jnp.exp(s_wn - m_w)
    l_w = jnp.sum(p_w, axis=-1, keepdims=True) + jnp.sum(p_wn, axis=-1, keepdims=True)
    acc_w = _dot_nt(_bf(p_w), _bf(win_ref[K_ROWS:KV_ROWS, :])) + _dot(_bf(p_wn), new_wv)
    o_w = fold(acc_w) / l_w

    gates = _sigmoid(gate_ref[...])
    o_ref[...] = gates[:, 0:1] * o_c + gates[:, 1:2] * o_s + gates[:, 2:3] * o_w


def _attn_sample(page_table, cache_pages, page_specs, q_s, win_state, new_kv, c_t, gates, e_mat, l,
                 n_b, n_s, past):
    n_pages = len(page_specs)
    page = LANE
    rows = N_KV * n_s * HEADS_PER_KV
    n_blk = c_t.shape[-1]
    wlen = win_state.shape[-1]
    body = functools.partial(_attn_sample_body, n_pages=n_pages, page=page, n_s=n_s, past=past)
    grid_spec = pltpu.PrefetchScalarGridSpec(
        num_scalar_prefetch=1,
        grid=(n_b,),
        in_specs=list(page_specs) + [
            pl.BlockSpec((None, rows, LANE), lambda b, pt: (b, 0, 0)),
            pl.BlockSpec((None, KV_ROWS, wlen), lambda b, pt: (l * n_b + b, 0, 0)),
            pl.BlockSpec((None, new_kv.shape[1], KV_W), lambda b, pt: (b, 0, 0)),
            pl.BlockSpec((None, 2, K_ROWS, n_blk), lambda b, pt: (b, 0, 0, 0)),
            pl.BlockSpec((None, rows, 3), lambda b, pt: (b, 0, 0)),
            pl.BlockSpec((LANE, n_pages * page), lambda b, pt: (0, 0)),
        ],
        out_specs=pl.BlockSpec((None, rows, LANE), lambda b, pt: (b, 0, 0)),
        scratch_shapes=[
            pltpu.VMEM((rows, n_pages * page), F32),
            pltpu.VMEM((176, LANE), F32),
        ],
    )
    return pl.pallas_call(
        body,
        grid_spec=grid_spec,
        out_shape=jax.ShapeDtypeStruct((n_b, rows, LANE), F32),
        compiler_params=_params(("arbitrary",)),
        name="attn_sample",
    )(page_table, *([cache_pages] * n_pages), q_s, win_state, new_kv, c_t, gates, e_mat)


def _merge_body(x_ref, m_ref, y0_ref, y1_ref, y2_ref, gm_ref, w_ref, g_ref, b_ref, o_ref, acc_scr,
                *, per_row, tiles_per_b, reps, alpha):
    i = pl.program_id(0)
    k = pl.program_id(1)

    def branch(y_ref, first):
        contrib = _sigmoid(gm_ref[...]) * _dot(_bf(y_ref[...]), _bf(w_ref[...]))
        if first:
            acc_scr[...] = contrib
        else:
            acc_scr[...] += contrib

    @pl.when(k == 0)
    def _():
        branch(y0_ref, True)

    @pl.when(k == 1)
    def _():
        branch(y1_ref, False)

    @pl.when(k == 2)
    def _():
        branch(y2_ref, False)

    @pl.when(k == 3)
    def _():
        out = _dot(_bf(acc_scr[...]), _bf(w_ref[...]))
        gate = _mod_rows(m_ref, 2, per_row, i // tiles_per_b, reps)
        z = alpha * x_ref[...] + gate * out
        o_ref[...] = _layer_norm(z, g_ref[...], b_ref[...])


def _merge(x, mods, l, y_rnn, y_pool, y_attn, proj_a, w_stack, ln_g, ln_b, *, per_row, tm, rows_per_b,
           mod_row_block, mod_rows, alpha):
    m_tok = x.shape[0]
    reps = tm // mod_rows if per_row else 1
    tiles_per_b = max(rows_per_b // tm, 1)
    body = functools.partial(_merge_body, per_row=per_row, tiles_per_b=tiles_per_b, reps=reps, alpha=alpha)
    y_spec = pl.BlockSpec((tm, D_MODEL), lambda i, k: (i, 0))
    return pl.pallas_call(
        body,
        grid=(m_tok // tm, 4),
        in_specs=[
            y_spec,
            pl.BlockSpec((None, 3, mod_rows, D_MODEL), lambda i, k: (l, 1, mod_row_block, 0)),
            y_spec, y_spec, y_spec,
            pl.BlockSpec((tm, D_MODEL), lambda i, k: (i, 3 + jnp.minimum(k, 2))),
            pl.BlockSpec((None, None, D_MODEL, D_MODEL), lambda i, k: (l, k, 0, 0)),
            pl.BlockSpec((None, None, 1, D_MODEL), lambda i, k: (l, 1, 0, 0)),
            pl.BlockSpec((None, None, 1, D_MODEL), lambda i, k: (l, 1, 0, 0)),
        ],
        out_specs=y_spec,
        out_shape=jax.ShapeDtypeStruct((m_tok, D_MODEL), F32),
        scratch_shapes=[pltpu.VMEM((tm, D_MODEL), F32)],
        compiler_params=_params(("arbitrary", "arbitrary")),
        name="merge",
    )(x, mods, y_rnn, y_pool, y_attn, proj_a, w_stack, ln_g, ln_b)


def _rope_tables(pos):
    inv = ROPE_THETA ** (-jnp.arange(ROT_HALF, dtype=F32) * (2.0 / ROT_DIM))
    ang = inv[:, None] * pos.astype(F32)[None, :]
    return jnp.cos(ang), jnp.sin(ang)


def _rope_lane_tables(cos, sin):
    d = np.arange(LANE) % HEAD_DIM
    idx = np.where(d < ROT_HALF, d, np.clip(d - ROT_HALF, 0, ROT_HALF - 1))
    cos_l = cos.T[:, idx]
    sin_l = sin.T[:, idx]
    first = jnp.asarray(d < ROT_HALF)[None, :]
    second = jnp.asarray((d >= ROT_HALF) & (d < ROT_DIM))[None, :]
    c_tab = jnp.where(first | second, cos_l, 1.0)
    s_lo = jnp.where(second, sin_l, 0.0)
    s_hi = jnp.where(first, -sin_l, 0.0)
    return c_tab, s_lo, s_hi


def kernel(x_prompt, x_sample, c_prompt, c_sample, cache_cmp_kv, cache_slc_kv, state_win_kv, state_rnn_h, state_rnn_conv, state_pool, page_table, ada_w, ada_b, ln_g, ln_b, ffn_w_up, ffn_w_down, w_in, b_in, rnn_conv_w, rnn_conv_b, rnn_gate_w, rnn_gate_b, rnn_lambda, pool_w, pool_scale, cmp_w1, cmp_b1, cmp_w2, cmp_pe, w_rnn_out, w_pool_out, w_attn_out, w_out):
    depth = ada_w.shape[0]
    n_bp, seq, _ = x_prompt.shape
    n_bs, n_s, _ = x_sample.shape
    n_pool, page = cache_cmp_kv.shape[1], cache_cmp_kv.shape[2]
    n_pages = page_table.shape[1]
    past = n_pages * page
    wlen = state_win_kv.shape[2]
    alpha = (2 * depth) ** 0.25
    assert page == LANE and seq == n_pages * page and n_bs == LANE

    c_all = jnp.concatenate([c_sample, c_prompt], axis=0)
    mods = _adaln(c_all, ada_w, ada_b)
    mod_p = dict(per_row=False, rows_per_b=seq, mod_row_block=n_bs // n_bp, mod_rows=n_bp)
    mod_s = dict(per_row=True, rows_per_b=n_bs * n_s, mod_row_block=0, mod_rows=n_bs)
    ln_g4 = ln_g.reshape(depth, 3, 1, D_MODEL)
    ln_b4 = ln_b.reshape(depth, 3, 1, D_MODEL)
    wt_full = jnp.swapaxes(w_in, 1, 2)
    c0 = 3 * D_MODEL + ATTN_W + KV_W
    c1 = c0 + 3 * N_HEADS
    pad_cols = PROJ_A_COLS - 6 * D_MODEL - 3 * N_HEADS
    wt_a = jnp.concatenate([wt_full[:, :3 * D_MODEL], wt_full[:, c1:], wt_full[:, c0:c1],
                            jnp.zeros((depth, pad_cols, D_MODEL), F32)], axis=1)
    bias_a = jnp.concatenate([b_in[:, :3 * D_MODEL], b_in[:, c1:], b_in[:, c0:c1],
                              jnp.zeros((depth, pad_cols), F32)], axis=1).reshape(depth, 1, PROJ_A_COLS)
    q0c = 3 * D_MODEL
    bias_q = b_in[:, q0c:q0c + ATTN_W].reshape(depth, 1, ATTN_W)
    bias_b = b_in[:, q0c + ATTN_W:q0c + ATTN_W + KV_W].reshape(depth, KV_W, 1)
    conv_b3 = rnn_conv_b.reshape(depth, 1, D_RNN)
    lam3 = rnn_lambda.reshape(depth, 1, D_RNN)
    per = 256 // RNN_BW
    gw = rnn_gate_w.reshape(depth, 2, D_RNN // 256, per, RNN_BW, RNN_BW)
    eye = jnp.eye(per, dtype=F32)
    wg = jnp.einsum('lcbpij,pq->lcbpiqj', gw, eye).reshape(depth, 2, D_RNN // 256, 256, 256)
    pool_scale3 = pool_scale.reshape(depth, 1, D_POOL)
    half = CMP_STRIDE * HEAD_DIM
    w1c = jnp.concatenate([cmp_w1[:, :, :half, :], cmp_w1[:, :, half:, :]], axis=-1)
    w2t = jnp.swapaxes(cmp_w2, 2, 3)
    pe_term = _pe_term(cmp_pe, cmp_w1, cmp_b1)
    w_stack = jnp.stack([w_rnn_out, w_pool_out, w_attn_out, w_out], axis=1)
    e_mat = (jnp.arange(LANE, dtype=jnp.int32)[:, None]
             == (jnp.arange(seq, dtype=jnp.int32)[None, :] // SLC_BLOCK)).astype(BF16)
    cos_p, sin_p = _rope_tables(jnp.arange(seq, dtype=jnp.int32))
    pos_s = past + jnp.repeat(jnp.arange(n_s, dtype=jnp.int32), n_bs)
    cos_s, sin_s = _rope_tables(pos_s)
    qtab_p = _rope_lane_tables(cos_p, sin_p)
    qtab_s = _rope_lane_tables(cos_s, sin_s)

    def kv_t(a):
        lead = a.shape[:-4]
        nd = len(lead)
        a = jnp.transpose(a, tuple(range(nd)) + (nd + 1, nd + 2, nd + 3, nd))
        return a.reshape(lead + (KV_ROWS, a.shape[-1]))

    cache_cmp_t = kv_t(cache_cmp_kv).reshape(depth * n_pool, KV_ROWS, page)
    cache_slc_t = kv_t(cache_slc_kv).reshape(depth * n_pool, KV_ROWS, page)
    win_t = kv_t(state_win_kv)
    win_flat = win_t.reshape(depth * n_bs, KV_ROWS, wlen)
    conv_state = jnp.swapaxes(state_rnn_conv, 1, 2)
    pool_state = jnp.swapaxes(state_pool, 1, 2)
    page_ident = jnp.arange(n_bp * n_pages, dtype=jnp.int32).reshape(n_bp, n_pages) % n_pages

    xp = x_prompt.reshape(n_bp * seq, D_MODEL)
    xs = jnp.swapaxes(x_sample, 0, 1).reshape(n_s * n_bs, D_MODEL)
    tm_p = 1024
    tm_s = n_s * n_bs

    outs = {k: [] for k in ("cmp_p", "cmp_s", "slc_p", "slc_s", "win_p", "win_s", "h_p", "h_s",
                            "conv_p", "conv_s", "pool_p", "pool_s")}
    for l in range(depth):
        xp = _ffn(xp, mods, l, 0, ffn_w_up, ffn_w_down, ln_g4, ln_b4, 0, tm=tm_p, alpha=alpha, **mod_p)
        xs = _ffn(xs, mods, l, 0, ffn_w_up, ffn_w_down, ln_g4, ln_b4, 0, tm=tm_s, alpha=alpha, **mod_s)

        pa_p = _proj_a(xp, mods, l, wt_a, bias_a, tm=tm_p, **mod_p)
        pa_s = _proj_a(xs, mods, l, wt_a, bias_a, tm=tm_s, **mod_s)
        kv_p = _proj_b(xp, mods, l, wt_full, bias_b, cos_p, sin_p, tm=tm_p, n_sub=1, **mod_p)
        kv_s = _proj_b(xs, mods, l, wt_full, bias_b, cos_s, sin_s, tm=tm_s, n_sub=n_s, **mod_s)
        q_p = _proj_q(xp, mods, l, wt_full, bias_q, qtab_p, tm=tm_p, **mod_p)
        q_s = _proj_q(xs, mods, l, wt_full, bias_q, qtab_s, tm=tm_s, **mod_s)

        y_rnn_p, hl_p, ul_p = _rglru_prompt(pa_p, l, n_bp, seq, rnn_conv_w, conv_b3, wg, rnn_gate_b, lam3)
        y_rnn_s, hn_s, cn_s = _rglru_sample(pa_s, l, n_bs, n_s, conv_state, state_rnn_h, rnn_conv_w,
                                            conv_b3, wg, rnn_gate_b, lam3)
        y_pool_p, last_p = _pool_prompt(pa_p, l, n_bp, seq, pool_w, pool_scale3)
        y_pool_s, pn_s = _pool_sample(pa_s, l, n_bs, n_s, pool_state, pool_w, pool_scale3)

        prompt_specs = [pl.BlockSpec((None, KV_ROWS, page), functools.partial(
            lambda b, pt, p: (b, 0, pt[b, p]), p=p)) for p in range(n_pages)]
        c_p = _compress(page_ident, kv_p, prompt_specs, n_bp, l, w1c, pe_term, w2t)
        cache_specs = [pl.BlockSpec((None, KV_ROWS, page), functools.partial(
            lambda b, pt, p: (l * n_pool + pt[b, p], 0, 0), p=p)) for p in range(n_pages)]
        c_s = _compress(page_table, cache_cmp_t, cache_specs, n_bs, l, w1c, pe_term, w2t)

        y_attn_p = _attn_prompt(q_p, kv_p, c_p, pa_p, e_mat, n_bp, seq)
        rows_s = N_KV * n_s * HEADS_PER_KV
        q_b = q_s.reshape(n_s, n_bs, N_KV, HEADS_PER_KV, HEAD_DIM)
        q_b = jnp.transpose(q_b, (1, 2, 0, 3, 4)).reshape(n_bs, rows_s, HEAD_DIM)
        q_b = jnp.concatenate([q_b, q_b], axis=-1)
        new_kv = jnp.transpose(kv_s, (2, 0, 1))
        new_kv = jnp.pad(new_kv, ((0, 0), (0, 16 - n_s), (0, 0)))
        g_s = pa_s[:, 6 * D_MODEL:6 * D_MODEL + 3 * N_HEADS].reshape(n_s, n_bs, N_KV, HEADS_PER_KV, 3)
        g_s = jnp.transpose(g_s, (1, 2, 0, 3, 4)).reshape(n_bs, rows_s, 3)
        slc_specs = [pl.BlockSpec((None, KV_ROWS, page), functools.partial(
            lambda b, pt, p: (l * n_pool + pt[b, p], 0, 0), p=p)) for p in range(n_pages)]
        o_s = _attn_sample(page_table, cache_slc_t, slc_specs, q_b, win_flat, new_kv, c_s, g_s, e_mat, l,
                           n_bs, n_s, past)
        y_attn_s = o_s[:, :, :HEAD_DIM].reshape(n_bs, N_KV, n_s, HEADS_PER_KV, HEAD_DIM)
        y_attn_s = jnp.transpose(y_attn_s, (2, 0, 1, 3, 4)).reshape(n_s * n_bs, ATTN_W)

        xp = _merge(xp, mods, l, y_rnn_p, y_pool_p, y_attn_p, pa_p, w_stack, ln_g4, ln_b4, tm=512,
                    alpha=alpha, **mod_p)
        xs = _merge(xs, mods, l, y_rnn_s, y_pool_s, y_attn_s, pa_s, w_stack, ln_g4, ln_b4, tm=tm_s,
                    alpha=alpha, **mod_s)

        xp = _ffn(xp, mods, l, 2, ffn_w_up, ffn_w_down, ln_g4, ln_b4, 2, tm=tm_p, alpha=alpha, **mod_p)
        xs = _ffn(xs, mods, l, 2, ffn_w_up, ffn_w_down, ln_g4, ln_b4, 2, tm=tm_s, alpha=alpha, **mod_s)

        kvp = kv_p.reshape(n_bp, 3, 2, N_KV, HEAD_DIM, seq)
        kvs = kv_s.reshape(n_s, 3, 2, N_KV, HEAD_DIM, n_bs)
        outs["cmp_p"].append(kvp[:, 0])
        outs["slc_p"].append(kvp[:, 1])
        outs["win_p"].append(kvp[:, 2, :, :, :, seq - min(WINDOW, seq):])
        outs["cmp_s"].append(kvs[:, 0])
        outs["slc_s"].append(kvs[:, 1])
        win_new = jnp.transpose(kvs[:, 2], (4, 1, 2, 3, 0))
        win_old = win_t[l].reshape(n_bs, 2, N_KV, HEAD_DIM, wlen)
        outs["win_s"].append(jnp.concatenate([win_old[..., n_s:], win_new], axis=-1))
        outs["h_p"].append(hl_p.reshape(n_bp, D_RNN))
        outs["h_s"].append(hn_s)
        outs["conv_p"].append(jnp.swapaxes(ul_p[:, 8 - (CONV_W - 1):, :], 0, 1))
        outs["conv_s"].append(cn_s)
        outs["pool_p"].append(jnp.swapaxes(last_p[:, 16 - POOL_BUF:, :], 0, 1))
        outs["pool_s"].append(pn_s)

    def kv_out_p(lst):
        return jnp.transpose(jnp.stack(lst), (0, 1, 5, 2, 3, 4))

    def kv_out_s(lst):
        return jnp.transpose(jnp.stack(lst), (0, 5, 1, 2, 3, 4))

    y_prompt = xp.reshape(n_bp, seq, D_MODEL)
    y_sample = jnp.swapaxes(xs.reshape(n_s, n_bs, D_MODEL), 0, 1)
    return (y_prompt, y_sample,
            kv_out_p(outs["cmp_p"]), kv_out_s(outs["cmp_s"]),
            kv_out_p(outs["slc_p"]), kv_out_s(outs["slc_s"]),
            kv_out_p(outs["win_p"]), kv_out_p(outs["win_s"]),
            jnp.stack(outs["h_p"]), jnp.stack(outs["h_s"]),
            jnp.swapaxes(jnp.stack(outs["conv_p"]), 1, 2), jnp.swapaxes(jnp.stack(outs["conv_s"]), 1, 2),
            jnp.swapaxes(jnp.stack(outs["pool_p"]), 1, 2), jnp.swapaxes(jnp.stack(outs["pool_s"]), 1, 2))
```

```python
import functools
import math

import jax
import jax.numpy as jnp
import numpy as np
from jax import lax
from jax.experimental import pallas as pl
from jax.experimental.pallas import tpu as pltpu

D_MODEL = 1024
N_HEADS = 16
N_KV = 4
HEADS_PER_KV = N_HEADS // N_KV
HEAD_DIM = 64
ROT_DIM = HEAD_DIM // 4
ROT_HALF = ROT_DIM // 2
ROPE_THETA = 500000.0
CMP_BLOCK = 32
CMP_STRIDE = 16
CMP_HIDDEN = 128
SLC_BLOCK = 64
N_SELECT = 8
WINDOW = 512
ATTN_SCALE = HEAD_DIM ** -0.5
D_RNN = D_MODEL
N_RNN_BLOCKS = 16
RNN_BW = D_RNN // N_RNN_BLOCKS
CONV_W = 4
RG_C = 8.0
D_POOL = D_MODEL
POOL_WINDOWS = (2, 4, 8, 16)
POOL_GW = D_POOL // len(POOL_WINDOWS)
POOL_BUF = max(POOL_WINDOWS) - 1
D_FF = 2816
ATTN_W = N_HEADS * HEAD_DIM
KV_W = 3 * 2 * N_KV * HEAD_DIM
LN_EPS = 1e-5

LANE = 128
KV_ROWS = 2 * N_KV * HEAD_DIM
K_ROWS = N_KV * HEAD_DIM
QKV_ROWS = ATTN_W + KV_W
PROJ_A_COLS = 3 * D_MODEL + 3 * D_MODEL + LANE
GATE_COL_BLOCK = (6 * D_MODEL) // LANE
NEG_BIG = -1e30
VMEM_LIMIT = 56 * 1024 * 1024

F32 = jnp.float32
BF16 = jnp.bfloat16


def _bf(x):
    return x.astype(BF16)


def _dot(a, b):
    return jnp.dot(a, b, preferred_element_type=F32)


def _dot_nt(a, b):
    return lax.dot_general(a, b, (((1,), (1,)), ((), ())), preferred_element_type=F32)


def _gelu(x):
    c = math.sqrt(2.0 / math.pi)
    return x * (0.5 * (1.0 + jnp.tanh(c * (x + 0.044715 * (x * x * x)))))


def _sigmoid(x):
    return 1.0 / (1.0 + jnp.exp(-x))


def _silu(x):
    return x * _sigmoid(x)


def _layer_norm(z, g, b):
    mu = jnp.mean(z, axis=-1, keepdims=True)
    zc = z - mu
    var = jnp.mean(zc * zc, axis=-1, keepdims=True)
    return zc * lax.rsqrt(var + LN_EPS) * g + b


def _params(sem):
    return pltpu.CompilerParams(dimension_semantics=sem, vmem_limit_bytes=VMEM_LIMIT)


def _mod_rows(m_ref, j, per_row, b_idx, reps):
    if per_row:
        m = m_ref[j]
        return jnp.concatenate([m] * reps, axis=0) if reps > 1 else m
    return m_ref[j, pl.ds(b_idx, 1), :]


def _adaln_body(c_ref, w_ref, b_ref, o_ref):
    sc = _silu(c_ref[...])
    o_ref[...] = _dot(_bf(sc), _bf(w_ref[...])) + b_ref[...]


def _adaln(c_all, ada_w, ada_b):
    depth = ada_w.shape[0]
    rows = c_all.shape[0]
    b4 = ada_b.reshape(depth, 9, 1, D_MODEL)
    return pl.pallas_call(
        _adaln_body,
        grid=(depth, 9),
        in_specs=[
            pl.BlockSpec((rows, D_MODEL), lambda l, k: (0, 0)),
            pl.BlockSpec((None, D_MODEL, D_MODEL), lambda l, k: (l, 0, k)),
            pl.BlockSpec((None, None, 1, D_MODEL), lambda l, k: (l, k, 0, 0)),
        ],
        out_specs=pl.BlockSpec((None, None, rows, D_MODEL), lambda l, k: (l, k, 0, 0)),
        out_shape=jax.ShapeDtypeStruct((depth, 9, rows, D_MODEL), F32),
        compiler_params=_params(("arbitrary", "arbitrary")),
        name="adaln",
    )(c_all, ada_w, b4)


def _ffn_body(x_ref, m_ref, wa_ref, wv_ref, wd_ref, g_ref, b_ref, o_ref, h_scr, acc_scr,
              *, per_row, tiles_per_b, reps, alpha):
    i = pl.program_id(0)
    j = pl.program_id(1)
    b_idx = i // tiles_per_b

    @pl.when(j == 0)
    def _():
        shift = _mod_rows(m_ref, 0, per_row, b_idx, reps)
        scale = _mod_rows(m_ref, 1, per_row, b_idx, reps)
        h_scr[...] = _bf(x_ref[...] * (1.0 + scale) + shift)
        acc_scr[...] = jnp.zeros_like(acc_scr)

    h = h_scr[...]
    a = _dot(h, _bf(wa_ref[...]))
    v = _dot(h, _bf(wv_ref[...]))
    act = _bf(_silu(a) * v)
    acc_scr[...] += _dot(act, _bf(wd_ref[...]))

    @pl.when(j == pl.num_programs(1) - 1)
    def _():
        gate = _mod_rows(m_ref, 2, per_row, b_idx, reps)
        z = alpha * x_ref[...] + 0.5 * gate * acc_scr[...]
        o_ref[...] = _layer_norm(z, g_ref[...], b_ref[...])


def _ffn(x, mods, l, k, w_up, w_down, ln_g, ln_b, ln_idx, *, per_row, tm, rows_per_b, mod_row_block,
         mod_rows, alpha):
    m_tok = x.shape[0]
    tf = 256
    n_f = D_FF // tf
    reps = tm // mod_rows if per_row else 1
    tiles_per_b = max(rows_per_b // tm, 1)
    body = functools.partial(_ffn_body, per_row=per_row, tiles_per_b=tiles_per_b, reps=reps, alpha=alpha)
    w_idx = 0 if k == 0 else 1
    return pl.pallas_call(
        body,
        grid=(m_tok // tm, n_f),
        in_specs=[
            pl.BlockSpec((tm, D_MODEL), lambda i, j: (i, 0)),
            pl.BlockSpec((None, 3, mod_rows, D_MODEL), lambda i, j: (l, k, mod_row_block, 0)),
            pl.BlockSpec((None, None, D_MODEL, tf), lambda i, j: (l, w_idx, 0, j)),
            pl.BlockSpec((None, None, D_MODEL, tf), lambda i, j: (l, w_idx, 0, n_f + j)),
            pl.BlockSpec((None, None, tf, D_MODEL), lambda i, j: (l, w_idx, j, 0)),
            pl.BlockSpec((None, None, 1, D_MODEL), lambda i, j: (l, ln_idx, 0, 0)),
            pl.BlockSpec((None, None, 1, D_MODEL), lambda i, j: (l, ln_idx, 0, 0)),
        ],
        out_specs=pl.BlockSpec((tm, D_MODEL), lambda i, j: (i, 0)),
        out_shape=jax.ShapeDtypeStruct((m_tok, D_MODEL), F32),
        scratch_shapes=[pltpu.VMEM((tm, D_MODEL), BF16), pltpu.VMEM((tm, D_MODEL), F32)],
        compiler_params=_params(("arbitrary", "arbitrary")),
        name="ffn",
    )(x, mods, w_up, w_up, w_down, ln_g, ln_b)


def _proj_a_body(x_ref, m_ref, w_ref, b_ref, o_ref, h_scr, *, per_row, tiles_per_b, reps):
    i = pl.program_id(0)
    j = pl.program_id(1)

    @pl.when(j == 0)
    def _():
        b_idx = i // tiles_per_b
        shift = _mod_rows(m_ref, 0, per_row, b_idx, reps)
        scale = _mod_rows(m_ref, 1, per_row, b_idx, reps)
        h_scr[...] = _bf(x_ref[...] * (1.0 + scale) + shift)

    o_ref[...] = _dot_nt(h_scr[...], _bf(w_ref[...])) + b_ref[...]


def _proj_a(x, mods, l, wt_a, bias_a, *, per_row, tm, rows_per_b, mod_row_block, mod_rows):
    m_tok = x.shape[0]
    tn = 896
    reps = tm // mod_rows if per_row else 1
    tiles_per_b = max(rows_per_b // tm, 1)
    body = functools.partial(_proj_a_body, per_row=per_row, tiles_per_b=tiles_per_b, reps=reps)
    return pl.pallas_call(
        body,
        grid=(m_tok // tm, PROJ_A_COLS // tn),
        in_specs=[
            pl.BlockSpec((tm, D_MODEL), lambda i, j: (i, 0)),
            pl.BlockSpec((None, 3, mod_rows, D_MODEL), lambda i, j: (l, 1, mod_row_block, 0)),
            pl.BlockSpec((None, tn, D_MODEL), lambda i, j: (l, j, 0)),
            pl.BlockSpec((None, 1, tn), lambda i, j: (l, 0, j)),
        ],
        out_specs=pl.BlockSpec((tm, tn), lambda i, j: (i, j)),
        out_shape=jax.ShapeDtypeStruct((m_tok, PROJ_A_COLS), F32),
        scratch_shapes=[pltpu.VMEM((tm, D_MODEL), BF16)],
        compiler_params=_params(("arbitrary", "arbitrary")),
        name="proj_a",
    )(x, mods, wt_a, bias_a)


def _rope_rows(res, cos, sin, row0):
    x1 = res[row0:row0 + ROT_HALF, :]
    x2 = res[row0 + ROT_HALF:row0 + ROT_DIM, :]
    return x1 * cos - x2 * sin, x2 * cos + x1 * sin


def _proj_b_body(x_ref, m_ref, w_ref, b_ref, cos_ref, sin_ref, o_ref, h_scr,
                 *, per_row, tiles_per_b, reps, n_sub, tf):
    i = pl.program_id(0)
    j = pl.program_id(1)

    @pl.when(j == 0)
    def _():
        b_idx = i // tiles_per_b
        shift = _mod_rows(m_ref, 0, per_row, b_idx, reps)
        scale = _mod_rows(m_ref, 1, per_row, b_idx, reps)
        h_scr[...] = _bf(x_ref[...] * (1.0 + scale) + shift)

    res = _dot_nt(_bf(w_ref[...]), h_scr[...]) + b_ref[...]
    sub = res.shape[1] // n_sub

    def store(rows, val):
        if n_sub == 1:
            o_ref[rows, :] = val
        else:
            for s in range(n_sub):
                o_ref[s, rows, :] = val[:, s * sub:(s + 1) * sub]

    store(slice(0, tf), res)

    @pl.when(j > 0)
    def _():
        cos = cos_ref[...]
        sin = sin_ref[...]
        for hh in range(N_KV):
            r0 = hh * HEAD_DIM
            n1, n2 = _rope_rows(res, cos, sin, r0)
            store(slice(r0, r0 + ROT_HALF), n1)
            store(slice(r0 + ROT_HALF, r0 + ROT_DIM), n2)


def _proj_b(x, mods, l, wt_full, bias_b, cos_t, sin_t, *, per_row, tm, rows_per_b, mod_row_block,
            mod_rows, n_sub):
    m_tok = x.shape[0]
    tf = KV_ROWS
    n_f = KV_W // tf
    row_block0 = (3 * D_MODEL + ATTN_W) // tf
    reps = tm // mod_rows if per_row else 1
    tiles_per_b = max(rows_per_b // tm, 1)
    body = functools.partial(_proj_b_body, per_row=per_row, tiles_per_b=tiles_per_b, reps=reps,
                             n_sub=n_sub, tf=tf)
    if n_sub == 1:
        n_b = m_tok // rows_per_b
        out_shape = jax.ShapeDtypeStruct((n_b, KV_W, rows_per_b), F32)
        out_spec = pl.BlockSpec((None, tf, tm), lambda i, j: (i // tiles_per_b, j, i % tiles_per_b))
        cs_spec = pl.BlockSpec((ROT_HALF, tm), lambda i, j: (0, i % tiles_per_b))
    else:
        sub = m_tok // n_sub
        out_shape = jax.ShapeDtypeStruct((n_sub, KV_W, sub), F32)
        out_spec = pl.BlockSpec((n_sub, tf, sub), lambda i, j: (0, j, 0))
        cs_spec = pl.BlockSpec((ROT_HALF, tm), lambda i, j: (0, 0))
    return pl.pallas_call(
        body,
        grid=(m_tok // tm, n_f),
        in_specs=[
            pl.BlockSpec((tm, D_MODEL), lambda i, j: (i, 0)),
            pl.BlockSpec((None, 3, mod_rows, D_MODEL), lambda i, j: (l, 1, mod_row_block, 0)),
            pl.BlockSpec((None, tf, D_MODEL), lambda i, j: (l, row_block0 + j, 0)),
            pl.BlockSpec((None, tf, 1), lambda i, j: (l, j, 0)),
            cs_spec,
            cs_spec,
        ],
        out_specs=out_spec,
        out_shape=out_shape,
        scratch_shapes=[pltpu.VMEM((tm, D_MODEL), BF16)],
        compiler_params=_params(("arbitrary", "arbitrary")),
        name="proj_b",
    )(x, mods, wt_full, bias_b, cos_t, sin_t)


def _proj_q_body(x_ref, m_ref, w_ref, b_ref, c_ref, slo_ref, shi_ref, o_ref, *, per_row, tiles_per_b, reps):
    i = pl.program_id(0)
    b_idx = i // tiles_per_b
    shift = _mod_rows(m_ref, 0, per_row, b_idx, reps)
    scale = _mod_rows(m_ref, 1, per_row, b_idx, reps)
    h = _bf(x_ref[...] * (1.0 + scale) + shift)
    q = _dot_nt(h, _bf(w_ref[...])) + b_ref[...]
    width = q.shape[1]
    n_rep = width // LANE
    cos = jnp.concatenate([c_ref[...]] * n_rep, axis=1)
    s_lo = jnp.concatenate([slo_ref[...]] * n_rep, axis=1)
    s_hi = jnp.concatenate([shi_ref[...]] * n_rep, axis=1)
    o_ref[...] = (q * cos + pltpu.roll(q, ROT_HALF, axis=1) * s_lo
                  + pltpu.roll(q, width - ROT_HALF, axis=1) * s_hi)


def _proj_q(x, mods, l, wt_full, bias_q, rope_tabs, *, per_row, tm, rows_per_b, mod_row_block, mod_rows):
    m_tok = x.shape[0]
    reps = tm // mod_rows if per_row else 1
    tiles_per_b = max(rows_per_b // tm, 1)
    row_block0 = (3 * D_MODEL) // ATTN_W
    body = functools.partial(_proj_q_body, per_row=per_row, tiles_per_b=tiles_per_b, reps=reps)
    if per_row:
        tab_spec = pl.BlockSpec((tm, LANE), lambda i: (0, 0))
    else:
        tab_spec = pl.BlockSpec((tm, LANE), lambda i: (i % tiles_per_b, 0))
    return pl.pallas_call(
        body,
        grid=(m_tok // tm,),
        in_specs=[
            pl.BlockSpec((tm, D_MODEL), lambda i: (i, 0)),
            pl.BlockSpec((None, 3, mod_rows, D_MODEL), lambda i: (l, 1, mod_row_block, 0)),
            pl.BlockSpec((None, ATTN_W, D_MODEL), lambda i: (l, row_block0, 0)),
            pl.BlockSpec((None, 1, ATTN_W), lambda i: (l, 0, 0)),
            tab_spec, tab_spec, tab_spec,
        ],
        out_specs=pl.BlockSpec((tm, ATTN_W), lambda i: (i, 0)),
        out_shape=jax.ShapeDtypeStruct((m_tok, ATTN_W), F32),
        compiler_params=_params(("arbitrary",)),
        name="proj_q",
    )(x, mods, wt_full, bias_q, *rope_tabs)


def _rglru_gates(xc, wg_ref, gb_ref, lam_ref):
    nblk = D_RNN // 256
    r_parts, i_parts = [], []
    for blk in range(nblk):
        xb = _bf(xc[:, blk * 256:(blk + 1) * 256])
        r_parts.append(_dot(xb, _bf(wg_ref[0, blk])))
        i_parts.append(_dot(xb, _bf(wg_ref[1, blk])))
    r = _sigmoid(jnp.concatenate(r_parts, axis=1) + gb_ref[0:1, :])
    ig = _sigmoid(jnp.concatenate(i_parts, axis=1) + gb_ref[1:2, :])
    lam = lam_ref[...]
    softplus_neg = jnp.maximum(-lam, 0.0) + jnp.log(1.0 + jnp.exp(-jnp.abs(lam)))
    log_a = -RG_C * r * softplus_neg
    a = jnp.exp(log_a)
    b = jnp.sqrt(1.0 - a * a) * (ig * xc)
    return a, b


def _rglru_prompt_body(u_ref, g_ref, cw_ref, cb_ref, wg_ref, gb_ref, lam_ref,
                       y_ref, hl_ref, ul_ref, h_scr, c_scr, *, tt):
    t = pl.program_id(1)

    @pl.when(t == 0)
    def _():
        h_scr[...] = jnp.zeros_like(h_scr)
        c_scr[...] = jnp.zeros_like(c_scr)

    u = u_ref[...]
    ext = jnp.concatenate([c_scr[...], u], axis=0)
    xc = cb_ref[...] + u * cw_ref[CONV_W - 1:CONV_W, :]
    for back in range(1, CONV_W):
        sh = pltpu.roll(ext, back, axis=0)[8:8 + tt]
        xc = xc + sh * cw_ref[CONV_W - 1 - back:CONV_W - back, :]
    a, b = _rglru_gates(xc, wg_ref, gb_ref, lam_ref)

    sub_row = lax.broadcasted_iota(jnp.int32, (tt, 1), 0) % 8
    d = 1
    while d < 8:
        a_sh = pltpu.roll(a, d, axis=0)
        b_sh = pltpu.roll(b, d, axis=0)
        valid = sub_row >= d
        b = jnp.where(valid, a * b_sh + b, b)
        a = jnp.where(valid, a * a_sh, a)
        d *= 2
    h_prev = h_scr[0:1, :]
    h_groups = []
    for k in range(tt // 8):
        h_k = b[8 * k:8 * k + 8, :] + a[8 * k:8 * k + 8, :] * h_prev
        h_groups.append(h_k)
        h_prev = h_k[7:8, :]
    h = jnp.concatenate(h_groups, axis=0)
    y_ref[...] = h * _gelu(g_ref[...])
    h_scr[...] = jnp.broadcast_to(h[tt - 1:tt, :], h_scr.shape)
    c_scr[...] = u[tt - 8:tt, :]

    @pl.when(t == pl.num_programs(1) - 1)
    def _():
        hl_ref[...] = h[tt - 1:tt, :]
        ul_ref[...] = u[tt - 8:tt, :]


def _rglru_prompt(proj_a, l, n_b, seq, conv_w, conv_b, wg, gate_b, lam):
    tt = 256
    nt = seq // tt
    body = functools.partial(_rglru_prompt_body, tt=tt)
    return pl.pallas_call(
        body,
        grid=(n_b, nt),
        in_specs=[
            pl.BlockSpec((tt, D_RNN), lambda b, t: (b * nt + t, 0)),
            pl.BlockSpec((tt, D_RNN), lambda b, t: (b * nt + t, 1)),
            pl.BlockSpec((None, CONV_W, D_RNN), lambda b, t: (l, 0, 0)),
            pl.BlockSpec((None, 1, D_RNN), lambda b, t: (l, 0, 0)),
            pl.BlockSpec((None, 2, D_RNN // 256, 256, 256), lambda b, t: (l, 0, 0, 0, 0)),
            pl.BlockSpec((None, 2, D_RNN), lambda b, t: (l, 0, 0)),
            pl.BlockSpec((None, 1, D_RNN), lambda b, t: (l, 0, 0)),
        ],
        out_specs=[
            pl.BlockSpec((tt, D_RNN), lambda b, t: (b * nt + t, 0)),
            pl.BlockSpec((None, 1, D_RNN), lambda b, t: (b, 0, 0)),
            pl.BlockSpec((None, 8, D_RNN), lambda b, t: (b, 0, 0)),
        ],
        out_shape=[
            jax.ShapeDtypeStruct((n_b * seq, D_RNN), F32),
            jax.ShapeDtypeStruct((n_b, 1, D_RNN), F32),
            jax.ShapeDtypeStruct((n_b, 8, D_RNN), F32),
        ],
        scratch_shapes=[pltpu.VMEM((8, D_RNN), F32), pltpu.VMEM((8, D_RNN), F32)],
        compiler_params=_params(("arbitrary", "arbitrary")),
        name="rglru_prompt",
    )(proj_a, proj_a, conv_w, conv_b, wg, gate_b, lam)


def _rglru_sample_body(u_ref, g_ref, buf_ref, h0_ref, cw_ref, cb_ref, wg_ref, gb_ref, lam_ref,
                       y_ref, hn_ref, cn_ref, *, n_b, n_s):
    u = u_ref[...]
    slabs = [buf_ref[r] for r in range(CONV_W - 1)] + [u[s * n_b:(s + 1) * n_b] for s in range(n_s)]
    xcs = []
    for s in range(n_s):
        xc = cb_ref[...] + slabs[s] * cw_ref[0:1, :]
        for k in range(1, CONV_W):
            xc = xc + slabs[s + k] * cw_ref[k:k + 1, :]
        xcs.append(xc)
    xc = jnp.concatenate(xcs, axis=0)
    a, b = _rglru_gates(xc, wg_ref, gb_ref, lam_ref)
    h = h0_ref[...]
    hs = []
    for s in range(n_s):
        h = a[s * n_b:(s + 1) * n_b] * h + b[s * n_b:(s + 1) * n_b]
        hs.append(h)
    y_ref[...] = jnp.concatenate(hs, axis=0) * _gelu(g_ref[...])
    hn_ref[...] = h
    for r in range(CONV_W - 1):
        cn_ref[r] = slabs[n_s + r]


def _rglru_sample(proj_a, l, n_b, n_s, conv_buf, h0, conv_w, conv_b, wg, gate_b, lam):
    m_tok = n_b * n_s
    body = functools.partial(_rglru_sample_body, n_b=n_b, n_s=n_s)
    return pl.pallas_call(
        body,
        grid=(1,),
        in_specs=[
            pl.BlockSpec((m_tok, D_RNN), lambda i: (0, 0)),
            pl.BlockSpec((m_tok, D_RNN), lambda i: (0, 1)),
            pl.BlockSpec((None, CONV_W - 1, n_b, D_RNN), lambda i: (l, 0, 0, 0)),
            pl.BlockSpec((None, n_b, D_RNN), lambda i: (l, 0, 0)),
            pl.BlockSpec((None, CONV_W, D_RNN), lambda i: (l, 0, 0)),
            pl.BlockSpec((None, 1, D_RNN), lambda i: (l, 0, 0)),
            pl.BlockSpec((None, 2, D_RNN // 256, 256, 256), lambda i: (l, 0, 0, 0, 0)),
            pl.BlockSpec((None, 2, D_RNN), lambda i: (l, 0, 0)),
            pl.BlockSpec((None, 1, D_RNN), lambda i: (l, 0, 0)),
        ],
        out_specs=[
            pl.BlockSpec((m_tok, D_RNN), lambda i: (0, 0)),
            pl.BlockSpec((n_b, D_RNN), lambda i: (0, 0)),
            pl.BlockSpec((CONV_W - 1, n_b, D_RNN), lambda i: (0, 0, 0)),
        ],
        out_shape=[
            jax.ShapeDtypeStruct((m_tok, D_RNN), F32),
            jax.ShapeDtypeStruct((n_b, D_RNN), F32),
            jax.ShapeDtypeStruct((CONV_W - 1, n_b, D_RNN), F32),
        ],
        compiler_params=_params(("arbitrary",)),
        name="rglru_sample",
    )(proj_a, proj_a, conv_buf, h0, conv_w, conv_b, wg, gate_b, lam)


def _pool_prompt_body(u_ref, w_ref, sc_ref, y_ref, last_ref, halo_scr, *, tt):
    t = pl.program_id(1)

    @pl.when(t == 0)
    def _():
        halo_scr[...] = jnp.zeros_like(halo_scr)

    x = u_ref[...]
    ext = jnp.concatenate([halo_scr[...], x], axis=0)
    pos = t * tt + lax.broadcasted_iota(jnp.int32, (tt, 1), 0)
    outs = []
    for gi, w in enumerate(POOL_WINDOWS):
        cols = slice(gi * POOL_GW, (gi + 1) * POOL_GW)
        s = ext[:, cols]
        k = 1
        while k < w:
            s = s + pltpu.roll(s, k, axis=0)
            k *= 2
        cnt = jnp.minimum(pos + 1, w).astype(F32)
        d = s[16:16 + tt] / cnt - x[:, cols]
        outs.append(_dot(_bf(d), _bf(w_ref[gi])))
    y_ref[...] = jnp.concatenate(outs, axis=1) * sc_ref[...]
    halo_scr[...] = x[tt - 16:tt, :]

    @pl.when(t == pl.num_programs(1) - 1)
    def _():
        last_ref[...] = x[tt - 16:tt, :]


def _pool_prompt(proj_a, l, n_b, seq, pool_w, pool_scale):
    tt = 256
    nt = seq // tt
    body = functools.partial(_pool_prompt_body, tt=tt)
    return pl.pallas_call(
        body,
        grid=(n_b, nt),
        in_specs=[
            pl.BlockSpec((tt, D_POOL), lambda b, t: (b * nt + t, 2)),
            pl.BlockSpec((None, len(POOL_WINDOWS), POOL_GW, POOL_GW), lambda b, t: (l, 0, 0, 0)),
            pl.BlockSpec((None, 1, D_POOL), lambda b, t: (l, 0, 0)),
        ],
        out_specs=[
            pl.BlockSpec((tt, D_POOL), lambda b, t: (b * nt + t, 0)),
            pl.BlockSpec((None, 16, D_POOL), lambda b, t: (b, 0, 0)),
        ],
        out_shape=[
            jax.ShapeDtypeStruct((n_b * seq, D_POOL), F32),
            jax.ShapeDtypeStruct((n_b, 16, D_POOL), F32),
        ],
        scratch_shapes=[pltpu.VMEM((16, D_POOL), F32)],
        compiler_params=_params(("arbitrary", "arbitrary")),
        name="pool_prompt",
    )(proj_a, pool_w, pool_scale)


def _pool_sample_body(u_ref, buf_ref, w_ref, sc_ref, y_ref, pn_ref, *, n_b, n_s):
    u = u_ref[...]
    slabs = [buf_ref[r] for r in range(POOL_BUF)] + [u[s * n_b:(s + 1) * n_b] for s in range(n_s)]
    outs = []
    for gi, w in enumerate(POOL_WINDOWS):
        cols = slice(gi * POOL_GW, (gi + 1) * POOL_GW)
        ds = []
        for s in range(n_s):
            acc = slabs[POOL_BUF + s][:, cols]
            for back in range(1, w):
                acc = acc + slabs[POOL_BUF + s - back][:, cols]
            ds.append(acc / float(w) - slabs[POOL_BUF + s][:, cols])
        outs.append(_dot(_bf(jnp.concatenate(ds, axis=0)), _bf(w_ref[gi])))
    y_ref[...] = jnp.concatenate(outs, axis=1) * sc_ref[...]
    for r in range(POOL_BUF):
        pn_ref[r] = slabs[n_s + r]


def _pool_sample(proj_a, l, n_b, n_s, pool_buf, pool_w, pool_scale):
    m_tok = n_b * n_s
    body = functools.partial(_pool_sample_body, n_b=n_b, n_s=n_s)
    return pl.pallas_call(
        body,
        grid=(1,),
        in_specs=[
            pl.BlockSpec((m_tok, D_POOL), lambda i: (0, 2)),
            pl.BlockSpec((None, POOL_BUF, n_b, D_POOL), lambda i: (l, 0, 0, 0)),
            pl.BlockSpec((None, len(POOL_WINDOWS), POOL_GW, POOL_GW), lambda i: (l, 0, 0, 0)),
            pl.BlockSpec((None, 1, D_POOL), lambda i: (l, 0, 0)),
        ],
        out_specs=[
            pl.BlockSpec((m_tok, D_POOL), lambda i: (0, 0)),
            pl.BlockSpec((POOL_BUF, n_b, D_POOL), lambda i: (0, 0, 0)),
        ],
        out_shape=[
            jax.ShapeDtypeStruct((m_tok, D_POOL), F32),
            jax.ShapeDtypeStruct((POOL_BUF, n_b, D_POOL), F32),
        ],
        compiler_params=_params(("arbitrary",)),
        name="pool_sample",
    )(proj_a, pool_buf, pool_w, pool_scale)


def _pe_term_body(pe_ref, w1_ref, b1_ref, o_ref):
    pe = jnp.broadcast_to(pe_ref[...], (8, pe_ref.shape[1]))
    o_ref[...] = _dot(_bf(pe), _bf(w1_ref[...]))[0:1, :] + b1_ref[...]


def _pe_term(cmp_pe, cmp_w1, cmp_b1):
    depth = cmp_pe.shape[0]
    flat = CMP_BLOCK * HEAD_DIM
    pe4 = cmp_pe.reshape(depth, 2, 1, flat)
    b4 = cmp_b1.reshape(depth, 2, 1, CMP_HIDDEN)
    return pl.pallas_call(
        _pe_term_body,
        grid=(depth, 2),
        in_specs=[
            pl.BlockSpec((None, None, 1, flat), lambda l, c: (l, c, 0, 0)),
            pl.BlockSpec((None, None, flat, CMP_HIDDEN), lambda l, c: (l, c, 0, 0)),
            pl.BlockSpec((None, None, 1, CMP_HIDDEN), lambda l, c: (l, c, 0, 0)),
        ],
        out_specs=pl.BlockSpec((None, None, 1, CMP_HIDDEN), lambda l, c: (l, c, 0, 0)),
        out_shape=jax.ShapeDtypeStruct((depth, 2, 1, CMP_HIDDEN), F32),
        compiler_params=_params(("arbitrary", "arbitrary")),
        name="pe_term",
    )(pe4, cmp_w1, b4)


def _compress_body(*refs, n_pages, page):
    pt_ref = refs[0]
    page_refs = refs[1:1 + n_pages]
    w1_ref, pet_ref, w2t_ref, o_ref, s_scr, x_scr = refs[1 + n_pages:]
    del pt_ref
    n_chunks = n_pages * page // CMP_STRIDE
    n_slab = KV_ROWS // LANE
    for p in range(n_pages):
        for j in range(n_slab):
            blk = page_refs[p][j * LANE:(j + 1) * LANE, :]
            s_scr[j, p * page:(p + 1) * page, :] = blk.T
    lane = lax.broadcasted_iota(jnp.int32, (n_chunks, LANE), 1)
    low = lane < HEAD_DIM
    for c in range(2):
        for gp in range(2):
            j = 2 * c + gp
            for r2 in range(CMP_STRIDE // 2):
                v0 = s_scr[j, pl.ds(2 * r2, n_chunks, stride=CMP_STRIDE), :]
                v1 = s_scr[j, pl.ds(2 * r2 + 1, n_chunks, stride=CMP_STRIDE), :]
                xe = jnp.where(low, v0, pltpu.roll(v1, HEAD_DIM, axis=1))
                xo = jnp.where(low, pltpu.roll(v0, HEAD_DIM, axis=1), v1)
                x_scr[(2 * gp) * n_chunks:(2 * gp + 1) * n_chunks, r2 * LANE:(r2 + 1) * LANE] = _bf(xe)
                x_scr[(2 * gp + 1) * n_chunks:(2 * gp + 2) * n_chunks, r2 * LANE:(r2 + 1) * LANE] = _bf(xo)
        pm = _dot(x_scr[...], _bf(w1_ref[c]))
        rows = pm.shape[0]
        pre = pm[:, :CMP_HIDDEN] + pltpu.roll(pm[:, CMP_HIDDEN:], rows - 1, axis=0)
        hid = _gelu(pre + pet_ref[c])
        out_t = _dot_nt(_bf(w2t_ref[c]), _bf(hid))
        for g in range(N_KV):
            o_ref[c, g * HEAD_DIM:(g + 1) * HEAD_DIM, :] = out_t[:, g * n_chunks:(g + 1) * n_chunks]


def _compress(page_table, src, page_specs, n_seq, l, w1c, pe_term, w2t):
    n_pages = len(page_specs)
    page = LANE
    n_chunks = n_pages * page // CMP_STRIDE
    body = functools.partial(_compress_body, n_pages=n_pages, page=page)
    grid_spec = pltpu.PrefetchScalarGridSpec(
        num_scalar_prefetch=1,
        grid=(n_seq,),
        in_specs=list(page_specs) + [
            pl.BlockSpec((None, 2, CMP_STRIDE * HEAD_DIM, 2 * CMP_HIDDEN), lambda b, pt: (l, 0, 0, 0)),
            pl.BlockSpec((None, 2, 1, CMP_HIDDEN), lambda b, pt: (l, 0, 0, 0)),
            pl.BlockSpec((None, 2, HEAD_DIM, CMP_HIDDEN), lambda b, pt: (l, 0, 0, 0)),
        ],
        out_specs=pl.BlockSpec((None, 2, K_ROWS, n_chunks), lambda b, pt: (b, 0, 0, 0)),
        scratch_shapes=[
            pltpu.VMEM((KV_ROWS // LANE, n_pages * page, LANE), F32),
            pltpu.VMEM((N_KV * n_chunks, CMP_STRIDE * HEAD_DIM), BF16),
        ],
    )
    return pl.pallas_call(
        body,
        grid_spec=grid_spec,
        out_shape=jax.ShapeDtypeStruct((n_seq, 2, K_ROWS, n_chunks), F32),
        compiler_params=_params(("arbitrary",)),
        name="compress",
    )(page_table, *([src] * n_pages), w1c, pe_term, w2t)


def _select_blocks(score_t, cur, n_blocks):
    rows = score_t.shape[0]
    j = lax.broadcasted_iota(jnp.int32, score_t.shape, 0)
    forced = (j == 0) | (j == cur) | (j == cur - 1)
    sc = jnp.where(j > cur, -jnp.inf, jnp.where(forced, jnp.inf, score_t))
    rank = jnp.zeros(score_t.shape, jnp.int32)
    for jp in range(n_blocks):
        row = sc[jp:jp + 1, :]
        ahead = (row > sc) | ((row == sc) & (jp < j))
        rank = rank + ahead.astype(jnp.int32)
    del rows
    return ((rank < N_SELECT) & (j <= cur)).astype(F32)


def _softmax_update(s, m, l_sum, acc, v_t):
    m_new = jnp.maximum(m, jnp.max(s, axis=-1, keepdims=True))
    alpha = jnp.exp(m - m_new)
    p = jnp.exp(s - m_new)
    l_new = alpha * l_sum + jnp.sum(p, axis=-1, keepdims=True)
    acc_new = alpha * acc + _dot_nt(_bf(p), v_t)
    return m_new, l_new, acc_new


def _stack_heads(k_t):
    return _bf(jnp.concatenate([k_t] * HEADS_PER_KV, axis=0))


def _attn_prompt_body(q_ref, sk_ref, sv_ref, wk_ref, wv_ref, kc_ref, vc_ref, gate_ref, e_ref,
                      o_ref, imp_scr, *, tq, kc, seq):
    g = pl.program_id(1)
    i = pl.program_id(2)
    q0 = i * tq
    hp = HEADS_PER_KV
    rows = hp * tq
    q_w = hp * HEAD_DIM
    q_blk = q_ref[...] * ATTN_SCALE
    lane_head = lax.broadcasted_iota(jnp.int32, (1, q_w), 1) // HEAD_DIM
    q_all = _bf(jnp.concatenate([jnp.where(lane_head == h, q_blk, 0.0) for h in range(hp)], axis=0))
    qpos = q0 + lax.broadcasted_iota(jnp.int32, (rows, 1), 0) % tq

    n_blk = kc_ref.shape[1]
    blk_end = lax.broadcasted_iota(jnp.int32, (1, n_blk), 1) * CMP_STRIDE + (CMP_BLOCK - 1)
    cmask = blk_end <= qpos
    s_c = jnp.where(cmask, _dot(q_all, _stack_heads(kc_ref[...])), NEG_BIG)
    m_c = jnp.max(s_c, axis=-1, keepdims=True)
    e_c = jnp.where(cmask, jnp.exp(s_c - m_c), 0.0)
    p_c = e_c / jnp.maximum(jnp.sum(e_c, axis=-1, keepdims=True), 1e-30)
    o_c = _dot_nt(_bf(p_c), _bf(vc_ref[...]))

    imp = p_c[0:tq]
    for h in range(1, hp):
        imp = imp + p_c[h * tq:(h + 1) * tq]
    imp_scr[0:8, :] = jnp.zeros((8, tq), F32)
    imp_scr[8:8 + n_blk, :] = imp.T
    n_sel_blocks = seq // SLC_BLOCK
    ratio = SLC_BLOCK // CMP_STRIDE
    score_t = imp_scr[pl.ds(7, n_sel_blocks, stride=ratio), :]
    for o in range(1, ratio + 1):
        score_t = score_t + imp_scr[pl.ds(7 + o, n_sel_blocks, stride=ratio), :]
    cur = (q0 + lax.broadcasted_iota(jnp.int32, (1, tq), 1)) // SLC_BLOCK
    sel_t = _select_blocks(score_t, cur, n_sel_blocks)
    sel_pad = jnp.concatenate([sel_t, jnp.zeros((LANE - n_sel_blocks, tq), F32)], axis=0)
    sel_q = _bf(sel_pad.T)

    def sel_step(c, carry):
        m, l_sum, acc = carry
        k0 = pl.multiple_of(c * kc, kc)
        picked = _dot(sel_q, e_ref[:, pl.ds(k0, kc)])
        key = k0 + lax.broadcasted_iota(jnp.int32, (1, kc), 1)
        picked = jnp.concatenate([picked] * hp, axis=0)
        ok = (picked > 0.5) & (key <= qpos)
        s = jnp.where(ok, _dot(q_all, _stack_heads(sk_ref[:, pl.ds(k0, kc)])), NEG_BIG)
        return _softmax_update(s, m, l_sum, acc, _bf(sv_ref[:, pl.ds(k0, kc)]))

    init = (jnp.full((rows, 1), NEG_BIG, F32), jnp.zeros((rows, 1), F32), jnp.zeros((rows, HEAD_DIM), F32))
    n_sel_chunks = (q0 + tq + kc - 1) // kc
    _, l_s, acc_s = lax.fori_loop(0, n_sel_chunks, sel_step, init)
    o_s = acc_s / l_s

    def win_step(c, carry):
        m, l_sum, acc = carry
        k0 = pl.multiple_of(c * kc, kc)
        key = k0 + lax.broadcasted_iota(jnp.int32, (1, kc), 1)
        dpos = qpos - key
        ok = (dpos >= 0) & (dpos < WINDOW)
        s = jnp.where(ok, _dot(q_all, _stack_heads(wk_ref[:, pl.ds(k0, kc)])), NEG_BIG)
        return _softmax_update(s, m, l_sum, acc, _bf(wv_ref[:, pl.ds(k0, kc)]))

    c_lo = jnp.maximum(q0 - WINDOW + 1, 0) // kc
    _, l_w, acc_w = lax.fori_loop(c_lo, n_sel_chunks, win_step, init)
    o_w = acc_w / l_w

    gates = _sigmoid(gate_ref[...])
    d_i = lax.broadcasted_iota(jnp.int32, (HEAD_DIM, q_w), 0)
    l_i = lax.broadcasted_iota(jnp.int32, (HEAD_DIM, q_w), 1)
    out = jnp.zeros((tq, q_w), F32)
    for h in range(hp):
        rs = slice(h * tq, (h + 1) * tq)
        acc = jnp.zeros((tq, HEAD_DIM), F32)
        for br, o_b in enumerate((o_c, o_s, o_w)):
            gsel = jnp.zeros((tq, 1), F32)
            for gg in range(N_KV):
                cidx = (gg * hp + h) * 3 + br
                gsel = gsel + jnp.where(g == gg, gates[:, cidx:cidx + 1], 0.0)
            acc = acc + gsel * o_b[rs]
        place = (l_i == d_i + h * HEAD_DIM).astype(BF16)
        out = out + _dot(_bf(acc), place)
    o_ref[...] = out


def _attn_prompt(q_tok, kv_t, c_t, proj_a, e_mat, n_b, seq):
    tq = 128
    kc = 256
    nq = seq // tq
    q_rows = HEADS_PER_KV * HEAD_DIM
    slc0 = KV_ROWS // HEAD_DIM
    win0 = 2 * KV_ROWS // HEAD_DIM
    body = functools.partial(_attn_prompt_body, tq=tq, kc=kc, seq=seq)
    n_blk = c_t.shape[-1]
    return pl.pallas_call(
        body,
        grid=(n_b, N_KV, nq),
        in_specs=[
            pl.BlockSpec((tq, q_rows), lambda b, g, i: (b * nq + i, g)),
            pl.BlockSpec((None, HEAD_DIM, seq), lambda b, g, i: (b, slc0 + g, 0)),
            pl.BlockSpec((None, HEAD_DIM, seq), lambda b, g, i: (b, slc0 + N_KV + g, 0)),
            pl.BlockSpec((None, HEAD_DIM, seq), lambda b, g, i: (b, win0 + g, 0)),
            pl.BlockSpec((None, HEAD_DIM, seq), lambda b, g, i: (b, win0 + N_KV + g, 0)),
            pl.BlockSpec((None, None, HEAD_DIM, n_blk), lambda b, g, i: (b, 0, g, 0)),
            pl.BlockSpec((None, None, HEAD_DIM, n_blk), lambda b, g, i: (b, 1, g, 0)),
            pl.BlockSpec((tq, LANE), lambda b, g, i: (b * nq + i, GATE_COL_BLOCK)),
            pl.BlockSpec((LANE, seq), lambda b, g, i: (0, 0)),
        ],
        out_specs=pl.BlockSpec((tq, q_rows), lambda b, g, i: (b * nq + i, g)),
        out_shape=jax.ShapeDtypeStruct((n_b * seq, ATTN_W), F32),
        scratch_shapes=[pltpu.VMEM((8 + n_blk + 8, tq), F32)],
        compiler_params=_params(("arbitrary", "arbitrary", "arbitrary")),
        name="attn_prompt",
    )(q_tok, kv_t, kv_t, kv_t, kv_t, c_t, c_t, proj_a, e_mat)


def _square_t(x):
    ra, cb = x.shape[0] // LANE, x.shape[1] // LANE
    rows = []
    for c in range(cb):
        rows.append(jnp.concatenate([x[r * LANE:(r + 1) * LANE, c * LANE:(c + 1) * LANE].T for r in range(ra)],
                                    axis=1) if ra > 1 else x[0:LANE, c * LANE:(c + 1) * LANE].T)
    return jnp.concatenate(rows, axis=0) if cb > 1 else rows[0]


def _rows_to_key_major(k_t):
    padded = jnp.concatenate([k_t, jnp.zeros((LANE - HEAD_DIM, k_t.shape[1]), F32)], axis=0)
    return _bf(padded.T)


def _col_softmax_update(s, m, l_sum, acc, v_t):
    m_new = jnp.maximum(m, jnp.max(s, axis=0, keepdims=True))
    alpha = jnp.exp(m - m_new)
    p = jnp.exp(s - m_new)
    l_new = alpha * l_sum + jnp.sum(p, axis=0, keepdims=True)
    acc_new = alpha * acc + _dot(v_t, _bf(p))
    return m_new, l_new, acc_new


def _attn_prompt_km_body(q_ref, sk_ref, sv_ref, wk_ref, wv_ref, kc_ref, vc_ref, gate_ref,
                         o_ref, ks_scr, kw_scr, imp_scr, sel_scr, g_scr, *, tq, ck, seq):
    g = pl.program_id(1)
    i = pl.program_id(2)
    q0 = i * tq
    hp = HEADS_PER_KV
    cols = hp * tq

    @pl.when(i == 0)
    def _():
        for t in range(seq // LANE):
            ks_scr[t * LANE:(t + 1) * LANE, :] = _rows_to_key_major(sk_ref[:, t * LANE:(t + 1) * LANE])
            kw_scr[t * LANE:(t + 1) * LANE, :] = _rows_to_key_major(wk_ref[:, t * LANE:(t + 1) * LANE])

    q_t = _square_t(q_ref[...] * ATTN_SCALE)
    q_cols = jnp.concatenate([q_t[h * HEAD_DIM:(h + 1) * HEAD_DIM, :] for h in range(hp)], axis=1)
    q_cols = _bf(jnp.concatenate([q_cols, jnp.zeros((LANE - HEAD_DIM, cols), F32)], axis=0))
    qpos = q0 + lax.broadcasted_iota(jnp.int32, (1, cols), 1) % tq
    qpos_q = q0 + lax.broadcasted_iota(jnp.int32, (1, tq), 1)

    n_blk = kc_ref.shape[1]
    s_c = _dot(_rows_to_key_major(kc_ref[...]), q_cols)
    blk_end = lax.broadcasted_iota(jnp.int32, (n_blk, 1), 0) * CMP_STRIDE + (CMP_BLOCK - 1)
    cmask = blk_end <= qpos
    s_c = jnp.where(cmask, s_c, NEG_BIG)
    m_c = jnp.max(s_c, axis=0, keepdims=True)
    e_c = jnp.where(cmask, jnp.exp(s_c - m_c), 0.0)
    p_c = e_c / jnp.maximum(jnp.sum(e_c, axis=0, keepdims=True), 1e-30)
    o_c = _dot(_bf(vc_ref[...]), _bf(p_c))

    imp_t = p_c[:, 0:tq]
    for h in range(1, hp):
        imp_t = imp_t + p_c[:, h * tq:(h + 1) * tq]
    imp_scr[0:8, :] = jnp.zeros((8, tq), F32)
    imp_scr[8:8 + n_blk, :] = imp_t
    n_sel_blocks = seq // SLC_BLOCK
    ratio = SLC_BLOCK // CMP_STRIDE
    score_t = imp_scr[pl.ds(7, n_sel_blocks, stride=ratio), :]
    for o in range(1, ratio + 1):
        score_t = score_t + imp_scr[pl.ds(7 + o, n_sel_blocks, stride=ratio), :]
    sel_scr[...] = _select_blocks(score_t, qpos_q // SLC_BLOCK, n_sel_blocks)

    sub = lax.broadcasted_iota(jnp.int32, (SLC_BLOCK, 1), 0)
    init = (jnp.full((1, cols), NEG_BIG, F32), jnp.zeros((1, cols), F32), jnp.zeros((HEAD_DIM, cols), F32))

    def sel_step(c, carry):
        k0 = pl.multiple_of(c * ck, ck)
        parts = []
        for jj in range(ck // SLC_BLOCK):
            picked = sel_scr[pl.ds(c * (ck // SLC_BLOCK) + jj, 1), :]
            key = k0 + jj * SLC_BLOCK + sub
            parts.append(jnp.where((picked > 0.5) & (key <= qpos_q), 0.0, NEG_BIG))
        bias = jnp.concatenate(parts, axis=0)
        s = _dot(ks_scr[pl.ds(k0, ck), :], q_cols) + jnp.concatenate([bias] * hp, axis=1)
        return _col_softmax_update(s, *carry, _bf(sv_ref[:, pl.ds(k0, ck)]))

    n_chunks = (q0 + tq + ck - 1) // ck
    _, l_s, acc_s = lax.fori_loop(0, n_chunks, sel_step, init)
    o_s = acc_s / l_s

    key_sub = lax.broadcasted_iota(jnp.int32, (ck, 1), 0)

    def win_step(c, carry):
        k0 = pl.multiple_of(c * ck, ck)
        dpos = qpos - (k0 + key_sub)
        bias = jnp.where((dpos >= 0) & (dpos < WINDOW), 0.0, NEG_BIG)
        s = _dot(kw_scr[pl.ds(k0, ck), :], q_cols) + bias
        return _col_softmax_update(s, *carry, _bf(wv_ref[:, pl.ds(k0, ck)]))

    c_lo = jnp.maximum(q0 - WINDOW + 1, 0) // ck
    _, l_w, acc_w = lax.fori_loop(c_lo, n_chunks, win_step, init)
    o_w = acc_w / l_w

    g_scr[...] = _sigmoid(gate_ref[...]).T
    heads = []
    for h in range(hp):
        cs = slice(h * tq, (h + 1) * tq)
        base = (g * hp + h) * 3
        acc = g_scr[pl.ds(base, 1), :] * o_c[:, cs]
        acc = acc + g_scr[pl.ds(base + 1, 1), :] * o_s[:, cs]
        acc = acc + g_scr[pl.ds(base + 2, 1), :] * o_w[:, cs]
        heads.append(acc)
    o_ref[...] = _square_t(jnp.concatenate(heads, axis=0))


def _attn_prompt_km(q_tok, kv_t, c_t, proj_a, n_b, seq):
    tq = LANE
    ck = 256
    nq = seq // tq
    q_w = HEADS_PER_KV * HEAD_DIM
    slc0 = KV_ROWS // HEAD_DIM
    win0 = 2 * KV_ROWS // HEAD_DIM
    body = functools.partial(_attn_prompt_km_body, tq=tq, ck=ck, seq=seq)
    n_blk = c_t.shape[-1]
    return pl.pallas_call(
        body,
        grid=(n_b, N_KV, nq),
        in_specs=[
            pl.BlockSpec((tq, q_w), lambda b, g, i: (b * nq + i, g)),
            pl.BlockSpec((None, HEAD_DIM, seq), lambda b, g, i: (b, slc0 + g, 0)),
            pl.BlockSpec((None, HEAD_DIM, seq), lambda b, g, i: (b, slc0 + N_KV + g, 0)),
            pl.BlockSpec((None, HEAD_DIM, seq), lambda b, g, i: (b, win0 + g, 0)),
            pl.BlockSpec((None, HEAD_DIM, seq), lambda b, g, i: (b, win0 + N_KV + g, 0)),
            pl.BlockSpec((None, None, HEAD_DIM, n_blk), lambda b, g, i: (b, 0, g, 0)),
            pl.BlockSpec((None, None, HEAD_DIM, n_blk), lambda b, g, i: (b, 1, g, 0)),
            pl.BlockSpec((tq, LANE), lambda b, g, i: (b * nq + i, GATE_COL_BLOCK)),
        ],
        out_specs=pl.BlockSpec((tq, q_w), lambda b, g, i: (b * nq + i, g)),
        out_shape=jax.ShapeDtypeStruct((n_b * seq, ATTN_W), F32),
        scratch_shapes=[
            pltpu.VMEM((seq, LANE), BF16),
            pltpu.VMEM((seq, LANE), BF16),
            pltpu.VMEM((8 + n_blk + 8, tq), F32),
            pltpu.VMEM((seq // SLC_BLOCK, tq), F32),
            pltpu.VMEM((LANE, tq), F32),
        ],
        compiler_params=_params(("arbitrary", "arbitrary", "arbitrary")),
        name="attn_prompt",
    )(q_tok, kv_t, kv_t, kv_t, kv_t, c_t, c_t, proj_a)


def _attn_sample_body(*refs, n_pages, page, n_s, past):
    pt_ref = refs[0]
    page_refs = refs[1:1 + n_pages]
    q_ref, win_ref, new_ref, c_ref, gate_ref, e_ref, o_ref, s_scr, imp_scr = refs[1 + n_pages:]
    del pt_ref
    hp = HEADS_PER_KV
    rpg = n_s * hp
    rows = N_KV * rpg
    n_keys = n_pages * page
    row_i = lax.broadcasted_iota(jnp.int32, (rows, 1), 0)
    row_g = row_i // rpg
    row_s = (row_i % rpg) // hp
    col_g = lax.broadcasted_iota(jnp.int32, (1, K_ROWS), 1) // HEAD_DIM
    q2 = q_ref[...] * ATTN_SCALE
    diag = row_g == col_g
    q_bd = _bf(jnp.where(diag, jnp.concatenate([q2] * (K_ROWS // LANE), axis=1), 0.0))

    def fold(o_full):
        z = jnp.where(diag, o_full, 0.0)
        w = z[:, 0:LANE]
        for blk in range(1, K_ROWS // LANE):
            w = w + z[:, blk * LANE:(blk + 1) * LANE]
        return w + pltpu.roll(w, HEAD_DIM, axis=1)

    n_blk = c_ref.shape[-1]
    blk_end = lax.broadcasted_iota(jnp.int32, (1, n_blk), 1) * CMP_STRIDE + (CMP_BLOCK - 1)
    cmask = blk_end <= past + row_s
    s_c = jnp.where(cmask, _dot(q_bd, _bf(c_ref[0])), NEG_BIG)
    m_c = jnp.max(s_c, axis=-1, keepdims=True)
    e_c = jnp.where(cmask, jnp.exp(s_c - m_c), 0.0)
    p_c = e_c / jnp.maximum(jnp.sum(e_c, axis=-1, keepdims=True), 1e-30)
    o_c = fold(_dot_nt(_bf(p_c), _bf(c_ref[1])))

    t1 = p_c + pltpu.roll(p_c, rows - 1, axis=0)
    imp = t1 + pltpu.roll(t1, rows - 2, axis=0)
    imp_pad = jnp.concatenate([imp, jnp.zeros((LANE - rows, n_blk), F32)], axis=0)
    imp_scr[...] = jnp.zeros(imp_scr.shape, F32)
    imp_scr[8:8 + n_blk, :] = imp_pad.T
    n_sel_rows = 40
    n_sel_blocks = (past + n_s + SLC_BLOCK - 1) // SLC_BLOCK
    ratio = SLC_BLOCK // CMP_STRIDE
    score_t = imp_scr[pl.ds(7, n_sel_rows, stride=ratio), :]
    for o in range(1, ratio + 1):
        score_t = score_t + imp_scr[pl.ds(7 + o, n_sel_rows, stride=ratio), :]
    lane_row = lax.broadcasted_iota(jnp.int32, (1, LANE), 1)
    cur = (past + (lane_row % rpg) // hp) // SLC_BLOCK
    sel_t = _select_blocks(score_t, cur, n_sel_blocks)
    sel_pad = jnp.concatenate([sel_t, jnp.zeros((LANE - n_sel_rows, LANE), F32)], axis=0)
    sel_rows = _bf(sel_pad.T)
    expand = (lax.broadcasted_iota(jnp.int32, (rows, LANE), 1) == (row_i // hp) * hp).astype(BF16)
    sel64 = _bf(_dot(expand, sel_rows))

    n_new = 16
    new_all = new_ref[...]
    new_sk = _bf(new_all[:, KV_ROWS:KV_ROWS + K_ROWS])
    new_sv = _bf(new_all[:, KV_ROWS + K_ROWS:2 * KV_ROWS])
    new_wk = _bf(new_all[:, 2 * KV_ROWS:2 * KV_ROWS + K_ROWS])
    new_wv = _bf(new_all[:, 2 * KV_ROWS + K_ROWS:3 * KV_ROWS])
    new_i = lax.broadcasted_iota(jnp.int32, (1, n_new), 1)
    new_ok = (new_i <= row_s) & (new_i < n_s)

    for p in range(n_pages):
        s_scr[:, p * page:(p + 1) * page] = _dot(q_bd, _bf(page_refs[p][0:K_ROWS, :]))
    picked = _dot(sel64, e_ref[...])
    s_all = jnp.where(picked > 0.5, s_scr[...], NEG_BIG)
    s_new = jnp.where(new_ok, _dot_nt(q_bd, new_sk), NEG_BIG)
    m_s = jnp.maximum(jnp.max(s_all, axis=-1, keepdims=True), jnp.max(s_new, axis=-1, keepdims=True))
    p_all = jnp.exp(s_all - m_s)
    p_new = jnp.exp(s_new - m_s)
    l_s = jnp.sum(p_all, axis=-1, keepdims=True) + jnp.sum(p_new, axis=-1, keepdims=True)
    acc = _dot(_bf(p_new), new_sv)
    p_all = _bf(p_all)
    for p in range(n_pages):
        acc = acc + _dot_nt(p_all[:, p * page:(p + 1) * page], _bf(page_refs[p][K_ROWS:KV_ROWS, :]))
    o_s = fold(acc) / l_s

    wlen = win_ref.shape[1]
    kk = lax.broadcasted_iota(jnp.int32, (1, wlen), 1)
    dpos = (past + row_s) - (past - wlen + kk)
    w_ok = (dpos >= 0) & (dpos < WINDOW)
    s_w = jnp.where(w_ok, _dot(q_bd, _bf(win_ref[0:K_ROWS, :])), NEG_BIG)
    s_wn = jnp.where(new_ok, _dot_nt(q_bd, new_wk), NEG_BIG)
    m_w = jnp.maximum(jnp.max(s_w, axis=-1, keepdims=True), jnp.max(s_wn, axis=-1, keepdims=True))
    p_w = jnp.exp(s_w - m_w)
    p_wn = jnp.exp(s_wn - m_w)
    l_w = jnp.sum(p_w, axis=-1, keepdims=True) + jnp.sum(p_wn, axis=-1, keepdims=True)
    acc_w = _dot_nt(_bf(p_w), _bf(win_ref[K_ROWS:KV_ROWS, :])) + _dot(_bf(p_wn), new_wv)
    o_w = fold(acc_w) / l_w

    gates = _sigmoid(gate_ref[...])
    o_ref[...] = gates[:, 0:1] * o_c + gates[:, 1:2] * o_s + gates[:, 2:3] * o_w


def _attn_sample(page_table, cache_pages, page_specs, q_s, win_state, new_kv, c_t, gates, e_mat, l,
                 n_b, n_s, past):
    n_pages = len(page_specs)
    page = LANE
    rows = N_KV * n_s * HEADS_PER_KV
    n_blk = c_t.shape[-1]
    wlen = win_state.shape[-1]
    body = functools.partial(_attn_sample_body, n_pages=n_pages, page=page, n_s=n_s, past=past)
    grid_spec = pltpu.PrefetchScalarGridSpec(
        num_scalar_prefetch=1,
        grid=(n_b,),
        in_specs=list(page_specs) + [
            pl.BlockSpec((None, rows, LANE), lambda b, pt: (b, 0, 0)),
            pl.BlockSpec((None, KV_ROWS, wlen), lambda b, pt: (l * n_b + b, 0, 0)),
            pl.BlockSpec((None, new_kv.shape[1], KV_W), lambda b, pt: (b, 0, 0)),
            pl.BlockSpec((None, 2, K_ROWS, n_blk), lambda b, pt: (b, 0, 0, 0)),
            pl.BlockSpec((None, rows, 3), lambda b, pt: (b, 0, 0)),
            pl.BlockSpec((LANE, n_pages * page), lambda b, pt: (0, 0)),
        ],
        out_specs=pl.BlockSpec((None, rows, LANE), lambda b, pt: (b, 0, 0)),
        scratch_shapes=[
            pltpu.VMEM((rows, n_pages * page), F32),
            pltpu.VMEM((176, LANE), F32),
        ],
    )
    return pl.pallas_call(
        body,
        grid_spec=grid_spec,
        out_shape=jax.ShapeDtypeStruct((n_b, rows, LANE), F32),
        compiler_params=_params(("arbitrary",)),
        name="attn_sample",
    )(page_table, *([cache_pages] * n_pages), q_s, win_state, new_kv, c_t, gates, e_mat)


def _merge_body(x_ref, m_ref, y0_ref, y1_ref, y2_ref, gm_ref, w_ref, g_ref, b_ref, o_ref, acc_scr,
                *, per_row, tiles_per_b, reps, alpha):
    i = pl.program_id(0)
    k = pl.program_id(1)

    def branch(y_ref, idx):
        contrib = _sigmoid(gm_ref[...]) * _dot(_bf(y_ref[...]), w_ref[idx])
        if idx == 0:
            acc_scr[...] = contrib
        else:
            acc_scr[...] += contrib

    @pl.when(k == 0)
    def _():
        branch(y0_ref, 0)

    @pl.when(k == 1)
    def _():
        branch(y1_ref, 1)

    @pl.when(k == 2)
    def _():
        branch(y2_ref, 2)

    @pl.when(k == 3)
    def _():
        out = _dot(_bf(acc_scr[...]), w_ref[3])
        gate = _mod_rows(m_ref, 2, per_row, i // tiles_per_b, reps)
        z = alpha * x_ref[...] + gate * out
        o_ref[...] = _layer_norm(z, g_ref[...], b_ref[...])


def _merge(x, mods, l, y_rnn, y_pool, y_attn, proj_a, w_stack, ln_g, ln_b, *, per_row, tm, rows_per_b,
           mod_row_block, mod_rows, alpha):
    m_tok = x.shape[0]
    reps = tm // mod_rows if per_row else 1
    tiles_per_b = max(rows_per_b // tm, 1)
    body = functools.partial(_merge_body, per_row=per_row, tiles_per_b=tiles_per_b, reps=reps, alpha=alpha)
    y_spec = pl.BlockSpec((tm, D_MODEL), lambda i, k: (i, 0))
    return pl.pallas_call(
        body,
        grid=(m_tok // tm, 4),
        in_specs=[
            y_spec,
            pl.BlockSpec((None, 3, mod_rows, D_MODEL), lambda i, k: (l, 1, mod_row_block, 0)),
            y_spec, y_spec, y_spec,
            pl.BlockSpec((tm, D_MODEL), lambda i, k: (i, 3 + jnp.minimum(k, 2))),
            pl.BlockSpec((None, 4, D_MODEL, D_MODEL), lambda i, k: (l, 0, 0, 0)),
            pl.BlockSpec((None, None, 1, D_MODEL), lambda i, k: (l, 1, 0, 0)),
            pl.BlockSpec((None, None, 1, D_MODEL), lambda i, k: (l, 1, 0, 0)),
        ],
        out_specs=y_spec,
        out_shape=jax.ShapeDtypeStruct((m_tok, D_MODEL), F32),
        scratch_shapes=[pltpu.VMEM((tm, D_MODEL), F32)],
        compiler_params=_params(("arbitrary", "arbitrary")),
        name="merge",
    )(x, mods, y_rnn, y_pool, y_attn, proj_a, w_stack, ln_g, ln_b)


def _rope_tables(pos):
    inv = ROPE_THETA ** (-jnp.arange(ROT_HALF, dtype=F32) * (2.0 / ROT_DIM))
    ang = inv[:, None] * pos.astype(F32)[None, :]
    return jnp.cos(ang), jnp.sin(ang)


def _rope_lane_tables(cos, sin):
    d = np.arange(LANE) % HEAD_DIM
    idx = np.where(d < ROT_HALF, d, np.clip(d - ROT_HALF, 0, ROT_HALF - 1))
    cos_l = cos.T[:, idx]
    sin_l = sin.T[:, idx]
    first = jnp.asarray(d < ROT_HALF)[None, :]
    second = jnp.asarray((d >= ROT_HALF) & (d < ROT_DIM))[None, :]
    c_tab = jnp.where(first | second, cos_l, 1.0)
    s_lo = jnp.where(second, sin_l, 0.0)
    s_hi = jnp.where(first, -sin_l, 0.0)
    return c_tab, s_lo, s_hi


def kernel(x_prompt, x_sample, c_prompt, c_sample, cache_cmp_kv, cache_slc_kv, state_win_kv, state_rnn_h, state_rnn_conv, state_pool, page_table, ada_w, ada_b, ln_g, ln_b, ffn_w_up, ffn_w_down, w_in, b_in, rnn_conv_w, rnn_conv_b, rnn_gate_w, rnn_gate_b, rnn_lambda, pool_w, pool_scale, cmp_w1, cmp_b1, cmp_w2, cmp_pe, w_rnn_out, w_pool_out, w_attn_out, w_out):
    depth = ada_w.shape[0]
    n_bp, seq, _ = x_prompt.shape
    n_bs, n_s, _ = x_sample.shape
    n_pool, page = cache_cmp_kv.shape[1], cache_cmp_kv.shape[2]
    n_pages = page_table.shape[1]
    past = n_pages * page
    wlen = state_win_kv.shape[2]
    alpha = (2 * depth) ** 0.25
    assert page == LANE and seq == n_pages * page and n_bs == LANE

    c_all = jnp.concatenate([c_sample, c_prompt], axis=0)
    mods = _adaln(c_all, ada_w, ada_b)
    mod_p = dict(per_row=False, rows_per_b=seq, mod_row_block=n_bs // n_bp, mod_rows=n_bp)
    mod_s = dict(per_row=True, rows_per_b=n_bs * n_s, mod_row_block=0, mod_rows=n_bs)
    ln_g4 = ln_g.reshape(depth, 3, 1, D_MODEL)
    ln_b4 = ln_b.reshape(depth, 3, 1, D_MODEL)
    wt_full = jnp.swapaxes(w_in, 1, 2)
    c0 = 3 * D_MODEL + ATTN_W + KV_W
    c1 = c0 + 3 * N_HEADS
    pad_cols = PROJ_A_COLS - 6 * D_MODEL - 3 * N_HEADS
    wt_a = jnp.concatenate([wt_full[:, :3 * D_MODEL], wt_full[:, c1:], wt_full[:, c0:c1],
                            jnp.zeros((depth, pad_cols, D_MODEL), F32)], axis=1)
    bias_a = jnp.concatenate([b_in[:, :3 * D_MODEL], b_in[:, c1:], b_in[:, c0:c1],
                              jnp.zeros((depth, pad_cols), F32)], axis=1).reshape(depth, 1, PROJ_A_COLS)
    q0c = 3 * D_MODEL
    bias_q = b_in[:, q0c:q0c + ATTN_W].reshape(depth, 1, ATTN_W)
    bias_b = b_in[:, q0c + ATTN_W:q0c + ATTN_W + KV_W].reshape(depth, KV_W, 1)
    conv_b3 = rnn_conv_b.reshape(depth, 1, D_RNN)
    lam3 = rnn_lambda.reshape(depth, 1, D_RNN)
    per = 256 // RNN_BW
    gw = rnn_gate_w.reshape(depth, 2, D_RNN // 256, per, RNN_BW, RNN_BW)
    eye = jnp.eye(per, dtype=F32)
    wg = jnp.einsum('lcbpij,pq->lcbpiqj', gw, eye).reshape(depth, 2, D_RNN // 256, 256, 256)
    pool_scale3 = pool_scale.reshape(depth, 1, D_POOL)
    half = CMP_STRIDE * HEAD_DIM
    w1c = jnp.concatenate([cmp_w1[:, :, :half, :], cmp_w1[:, :, half:, :]], axis=-1)
    w2t = jnp.swapaxes(cmp_w2, 2, 3)
    pe_term = _pe_term(cmp_pe, cmp_w1, cmp_b1)
    w_stack = _bf(jnp.stack([w_rnn_out, w_pool_out, w_attn_out, w_out], axis=1))
    e_mat = (jnp.arange(LANE, dtype=jnp.int32)[:, None]
             == (jnp.arange(seq, dtype=jnp.int32)[None, :] // SLC_BLOCK)).astype(BF16)
    cos_p, sin_p = _rope_tables(jnp.arange(seq, dtype=jnp.int32))
    pos_s = past + jnp.repeat(jnp.arange(n_s, dtype=jnp.int32), n_bs)
    cos_s, sin_s = _rope_tables(pos_s)
    qtab_p = _rope_lane_tables(cos_p, sin_p)
    qtab_s = _rope_lane_tables(cos_s, sin_s)

    def kv_t(a):
        lead = a.shape[:-4]
        nd = len(lead)
        a = jnp.transpose(a, tuple(range(nd)) + (nd + 1, nd + 2, nd + 3, nd))
        return a.reshape(lead + (KV_ROWS, a.shape[-1]))

    cache_cmp_t = kv_t(cache_cmp_kv).reshape(depth * n_pool, KV_ROWS, page)
    cache_slc_t = kv_t(cache_slc_kv).reshape(depth * n_pool, KV_ROWS, page)
    win_t = kv_t(state_win_kv)
    win_flat = win_t.reshape(depth * n_bs, KV_ROWS, wlen)
    conv_state = jnp.swapaxes(state_rnn_conv, 1, 2)
    pool_state = jnp.swapaxes(state_pool, 1, 2)
    page_ident = jnp.arange(n_bp * n_pages, dtype=jnp.int32).reshape(n_bp, n_pages) % n_pages

    xp = x_prompt.reshape(n_bp * seq, D_MODEL)
    xs = jnp.swapaxes(x_sample, 0, 1).reshape(n_s * n_bs, D_MODEL)
    tm_p = 1024
    tm_s = n_s * n_bs

    outs = {k: [] for k in ("cmp_p", "cmp_s", "slc_p", "slc_s", "win_p", "win_s", "h_p", "h_s",
                            "conv_p", "conv_s", "pool_p", "pool_s")}
    for l in range(depth):
        xp = _ffn(xp, mods, l, 0, ffn_w_up, ffn_w_down, ln_g4, ln_b4, 0, tm=tm_p, alpha=alpha, **mod_p)
        xs = _ffn(xs, mods, l, 0, ffn_w_up, ffn_w_down, ln_g4, ln_b4, 0, tm=tm_s, alpha=alpha, **mod_s)

        pa_p = _proj_a(xp, mods, l, wt_a, bias_a, tm=tm_p, **mod_p)
        pa_s = _proj_a(xs, mods, l, wt_a, bias_a, tm=tm_s, **mod_s)
        kv_p = _proj_b(xp, mods, l, wt_full, bias_b, cos_p, sin_p, tm=tm_p, n_sub=1, **mod_p)
        kv_s = _proj_b(xs, mods, l, wt_full, bias_b, cos_s, sin_s, tm=tm_s, n_sub=n_s, **mod_s)
        q_p = _proj_q(xp, mods, l, wt_full, bias_q, qtab_p, tm=tm_p, **mod_p)
        q_s = _proj_q(xs, mods, l, wt_full, bias_q, qtab_s, tm=tm_s, **mod_s)

        y_rnn_p, hl_p, ul_p = _rglru_prompt(pa_p, l, n_bp, seq, rnn_conv_w, conv_b3, wg, rnn_gate_b, lam3)
        y_rnn_s, hn_s, cn_s = _rglru_sample(pa_s, l, n_bs, n_s, conv_state, state_rnn_h, rnn_conv_w,
                                            conv_b3, wg, rnn_gate_b, lam3)
        y_pool_p, last_p = _pool_prompt(pa_p, l, n_bp, seq, pool_w, pool_scale3)
        y_pool_s, pn_s = _pool_sample(pa_s, l, n_bs, n_s, pool_state, pool_w, pool_scale3)

        prompt_specs = [pl.BlockSpec((None, KV_ROWS, page), functools.partial(
            lambda b, pt, p: (b, 0, pt[b, p]), p=p)) for p in range(n_pages)]
        c_p = _compress(page_ident, kv_p, prompt_specs, n_bp, l, w1c, pe_term, w2t)
        cache_specs = [pl.BlockSpec((None, KV_ROWS, page), functools.partial(
            lambda b, pt, p: (l * n_pool + pt[b, p], 0, 0), p=p)) for p in range(n_pages)]
        c_s = _compress(page_table, cache_cmp_t, cache_specs, n_bs, l, w1c, pe_term, w2t)

        y_attn_p = _attn_prompt_km(q_p, kv_p, c_p, pa_p, n_bp, seq)
        rows_s = N_KV * n_s * HEADS_PER_KV
        q_b = q_s.reshape(n_s, n_bs, N_KV, HEADS_PER_KV, HEAD_DIM)
        q_b = jnp.transpose(q_b, (1, 2, 0, 3, 4)).reshape(n_bs, rows_s, HEAD_DIM)
        q_b = jnp.concatenate([q_b, q_b], axis=-1)
        new_kv = jnp.transpose(kv_s, (2, 0, 1))
        new_kv = jnp.pad(new_kv, ((0, 0), (0, 16 - n_s), (0, 0)))
        g_s = pa_s[:, 6 * D_MODEL:6 * D_MODEL + 3 * N_HEADS].reshape(n_s, n_bs, N_KV, HEADS_PER_KV, 3)
        g_s = jnp.transpose(g_s, (1, 2, 0, 3, 4)).reshape(n_bs, rows_s, 3)
        slc_specs = [pl.BlockSpec((None, KV_ROWS, page), functools.partial(
            lambda b, pt, p: (l * n_pool + pt[b, p], 0, 0), p=p)) for p in range(n_pages)]
        o_s = _attn_sample(page_table, cache_slc_t, slc_specs, q_b, win_flat, new_kv, c_s, g_s, e_mat, l,
                           n_bs, n_s, past)
        y_attn_s = o_s[:, :, :HEAD_DIM].reshape(n_bs, N_KV, n_s, HEADS_PER_KV, HEAD_DIM)
        y_attn_s = jnp.transpose(y_attn_s, (2, 0, 1, 3, 4)).reshape(n_s * n_bs, ATTN_W)

        xp = _merge(xp, mods, l, y_rnn_p, y_pool_p, y_attn_p, pa_p, w_stack, ln_g4, ln_b4, tm=512,
                    alpha=alpha, **mod_p)
        xs = _merge(xs, mods, l, y_rnn_s, y_pool_s, y_attn_s, pa_s, w_stack, ln_g4, ln_b4, tm=tm_s,
                    alpha=alpha, **mod_s)

        xp = _ffn(xp, mods, l, 2, ffn_w_up, ffn_w_down, ln_g4, ln_b4, 2, tm=tm_p, alpha=alpha, **mod_p)
        xs = _ffn(xs, mods, l, 2, ffn_w_up, ffn_w_down, ln_g4, ln_b4, 2, tm=tm_s, alpha=alpha, **mod_s)

        kvp = kv_p.reshape(n_bp, 3, 2, N_KV, HEAD_DIM, seq)
        kvs = kv_s.reshape(n_s, 3, 2, N_KV, HEAD_DIM, n_bs)
        outs["cmp_p"].append(kvp[:, 0])
        outs["slc_p"].append(kvp[:, 1])
        outs["win_p"].append(kvp[:, 2, :, :, :, seq - min(WINDOW, seq):])
        outs["cmp_s"].append(kvs[:, 0])
        outs["slc_s"].append(kvs[:, 1])
        outs["win_s"].append(jnp.transpose(kvs[:, 2], (4, 1, 2, 3, 0)))
        outs["h_p"].append(hl_p.reshape(n_bp, D_RNN))
        outs["h_s"].append(hn_s)
        outs["conv_p"].append(jnp.swapaxes(ul_p[:, 8 - (CONV_W - 1):, :], 0, 1))
        outs["conv_s"].append(cn_s)
        outs["pool_p"].append(jnp.swapaxes(last_p[:, 16 - POOL_BUF:, :], 0, 1))
        outs["pool_s"].append(pn_s)

    def kv_out_p(lst):
        return jnp.transpose(jnp.stack(lst), (0, 1, 5, 2, 3, 4))

    def kv_out_s(lst):
        return jnp.transpose(jnp.stack(lst), (0, 5, 1, 2, 3, 4))

    win_old = win_t.reshape(depth, n_bs, 2, N_KV, HEAD_DIM, wlen)
    win_s_t = jnp.concatenate([win_old[..., n_s:], jnp.stack(outs["win_s"])], axis=-1)
    win_s_out = jnp.transpose(win_s_t, (0, 1, 5, 2, 3, 4))

    y_prompt = xp.reshape(n_bp, seq, D_MODEL)
    y_sample = jnp.swapaxes(xs.reshape(n_s, n_bs, D_MODEL), 0, 1)
    return (y_prompt, y_sample,
            kv_out_p(outs["cmp_p"]), kv_out_s(outs["cmp_s"]),
            kv_out_p(outs["slc_p"]), kv_out_s(outs["slc_s"]),
            kv_out_p(outs["win_p"]), win_s_out,
            jnp.stack(outs["h_p"]), jnp.stack(outs["h_s"]),
            jnp.swapaxes(jnp.stack(outs["conv_p"]), 1, 2), jnp.swapaxes(jnp.stack(outs["conv_s"]), 1, 2),
            jnp.swapaxes(jnp.stack(outs["pool_p"]), 1, 2), jnp.swapaxes(jnp.stack(outs["pool_s"]), 1, 2))
```

```python
import functools
import math

import jax
import jax.numpy as jnp
import numpy as np
from jax import lax
from jax.experimental import pallas as pl
from jax.experimental.pallas import tpu as pltpu

D_MODEL = 1024
N_HEADS = 16
N_KV = 4
HEADS_PER_KV = N_HEADS // N_KV
HEAD_DIM = 64
ROT_DIM = HEAD_DIM // 4
ROT_HALF = ROT_DIM // 2
ROPE_THETA = 500000.0
CMP_BLOCK = 32
CMP_STRIDE = 16
CMP_HIDDEN = 128
SLC_BLOCK = 64
N_SELECT = 8
WINDOW = 512
ATTN_SCALE = HEAD_DIM ** -0.5
D_RNN = D_MODEL
N_RNN_BLOCKS = 16
RNN_BW = D_RNN // N_RNN_BLOCKS
CONV_W = 4
RG_C = 8.0
D_POOL = D_MODEL
POOL_WINDOWS = (2, 4, 8, 16)
POOL_GW = D_POOL // len(POOL_WINDOWS)
POOL_BUF = max(POOL_WINDOWS) - 1
D_FF = 2816
ATTN_W = N_HEADS * HEAD_DIM
KV_W = 3 * 2 * N_KV * HEAD_DIM
LN_EPS = 1e-5

LANE = 128
KV_ROWS = 2 * N_KV * HEAD_DIM
K_ROWS = N_KV * HEAD_DIM
QKV_ROWS = ATTN_W + KV_W
PROJ_A_COLS = 3 * D_MODEL + 3 * D_MODEL + LANE
GATE_COL_BLOCK = (6 * D_MODEL) // LANE
NEG_BIG = -1e30
VMEM_LIMIT = 56 * 1024 * 1024

F32 = jnp.float32
BF16 = jnp.bfloat16


def _bf(x):
    return x.astype(BF16)


def _dot(a, b):
    return jnp.dot(a, b, preferred_element_type=F32)


def _dot_nt(a, b):
    return lax.dot_general(a, b, (((1,), (1,)), ((), ())), preferred_element_type=F32)


def _gelu(x):
    c = math.sqrt(2.0 / math.pi)
    return x * (0.5 * (1.0 + jnp.tanh(c * (x + 0.044715 * (x * x * x)))))


def _sigmoid(x):
    return 1.0 / (1.0 + jnp.exp(-x))


def _sigmoid_t(x):
    return 0.5 * jnp.tanh(0.5 * x) + 0.5


def _silu(x):
    return x * _sigmoid(x)


def _layer_norm(z, g, b):
    mu = jnp.mean(z, axis=-1, keepdims=True)
    zc = z - mu
    var = jnp.mean(zc * zc, axis=-1, keepdims=True)
    return zc * lax.rsqrt(var + LN_EPS) * g + b


def _params(sem):
    return pltpu.CompilerParams(dimension_semantics=sem, vmem_limit_bytes=VMEM_LIMIT)


def _mod_rows(m_ref, j, per_row, b_idx, reps):
    if per_row:
        m = m_ref[j]
        return jnp.concatenate([m] * reps, axis=0) if reps > 1 else m
    return m_ref[j, pl.ds(b_idx, 1), :]


def _adaln_body(c_ref, w_ref, b_ref, o_ref):
    sc = _silu(c_ref[...])
    o_ref[...] = _dot(_bf(sc), _bf(w_ref[...])) + b_ref[...]


def _adaln(c_all, ada_w, ada_b):
    depth = ada_w.shape[0]
    rows = c_all.shape[0]
    b4 = ada_b.reshape(depth, 9, 1, D_MODEL)
    return pl.pallas_call(
        _adaln_body,
        grid=(depth, 9),
        in_specs=[
            pl.BlockSpec((rows, D_MODEL), lambda l, k: (0, 0)),
            pl.BlockSpec((None, D_MODEL, D_MODEL), lambda l, k: (l, 0, k)),
            pl.BlockSpec((None, None, 1, D_MODEL), lambda l, k: (l, k, 0, 0)),
        ],
        out_specs=pl.BlockSpec((None, None, rows, D_MODEL), lambda l, k: (l, k, 0, 0)),
        out_shape=jax.ShapeDtypeStruct((depth, 9, rows, D_MODEL), F32),
        compiler_params=_params(("arbitrary", "arbitrary")),
        name="adaln",
    )(c_all, ada_w, b4)


def _ffn_body(x_ref, m_ref, wa_ref, wv_ref, wd_ref, g_ref, b_ref, o_ref, h_scr, acc_scr,
              *, per_row, tiles_per_b, reps, alpha):
    i = pl.program_id(0)
    j = pl.program_id(1)
    b_idx = i // tiles_per_b

    @pl.when(j == 0)
    def _():
        shift = _mod_rows(m_ref, 0, per_row, b_idx, reps)
        scale = _mod_rows(m_ref, 1, per_row, b_idx, reps)
        h_scr[...] = _bf(x_ref[...] * (1.0 + scale) + shift)
        acc_scr[...] = jnp.zeros_like(acc_scr)

    h = h_scr[...]
    a = _dot(h, _bf(wa_ref[...]))
    v = _dot(h, _bf(wv_ref[...]))
    act = _bf(_silu(a) * v)
    acc_scr[...] += _dot(act, _bf(wd_ref[...]))

    @pl.when(j == pl.num_programs(1) - 1)
    def _():
        gate = _mod_rows(m_ref, 2, per_row, b_idx, reps)
        z = alpha * x_ref[...] + 0.5 * gate * acc_scr[...]
        o_ref[...] = _layer_norm(z, g_ref[...], b_ref[...])


def _ffn(x, mods, l, k, w_up, w_down, ln_g, ln_b, ln_idx, *, per_row, tm, rows_per_b, mod_row_block,
         mod_rows, alpha):
    m_tok = x.shape[0]
    tf = 256
    n_f = D_FF // tf
    reps = tm // mod_rows if per_row else 1
    tiles_per_b = max(rows_per_b // tm, 1)
    body = functools.partial(_ffn_body, per_row=per_row, tiles_per_b=tiles_per_b, reps=reps, alpha=alpha)
    w_idx = 0 if k == 0 else 1
    return pl.pallas_call(
        body,
        grid=(m_tok // tm, n_f),
        in_specs=[
            pl.BlockSpec((tm, D_MODEL), lambda i, j: (i, 0)),
            pl.BlockSpec((None, 3, mod_rows, D_MODEL), lambda i, j: (l, k, mod_row_block, 0)),
            pl.BlockSpec((None, None, D_MODEL, tf), lambda i, j: (l, w_idx, 0, j)),
            pl.BlockSpec((None, None, D_MODEL, tf), lambda i, j: (l, w_idx, 0, n_f + j)),
            pl.BlockSpec((None, None, tf, D_MODEL), lambda i, j: (l, w_idx, j, 0)),
            pl.BlockSpec((None, None, 1, D_MODEL), lambda i, j: (l, ln_idx, 0, 0)),
            pl.BlockSpec((None, None, 1, D_MODEL), lambda i, j: (l, ln_idx, 0, 0)),
        ],
        out_specs=pl.BlockSpec((tm, D_MODEL), lambda i, j: (i, 0)),
        out_shape=jax.ShapeDtypeStruct((m_tok, D_MODEL), F32),
        scratch_shapes=[pltpu.VMEM((tm, D_MODEL), BF16), pltpu.VMEM((tm, D_MODEL), F32)],
        compiler_params=_params(("arbitrary", "arbitrary")),
        name="ffn",
    )(x, mods, w_up, w_up, w_down, ln_g, ln_b)


def _proj_a_body(x_ref, m_ref, w_ref, b_ref, o_ref, h_scr, *, per_row, tiles_per_b, reps):
    i = pl.program_id(0)
    j = pl.program_id(1)

    @pl.when(j == 0)
    def _():
        b_idx = i // tiles_per_b
        shift = _mod_rows(m_ref, 0, per_row, b_idx, reps)
        scale = _mod_rows(m_ref, 1, per_row, b_idx, reps)
        h_scr[...] = _bf(x_ref[...] * (1.0 + scale) + shift)

    o_ref[...] = _dot_nt(h_scr[...], _bf(w_ref[...])) + b_ref[...]


def _proj_a(x, mods, l, wt_a, bias_a, *, per_row, tm, rows_per_b, mod_row_block, mod_rows):
    m_tok = x.shape[0]
    tn = 896
    reps = tm // mod_rows if per_row else 1
    tiles_per_b = max(rows_per_b // tm, 1)
    body = functools.partial(_proj_a_body, per_row=per_row, tiles_per_b=tiles_per_b, reps=reps)
    return pl.pallas_call(
        body,
        grid=(m_tok // tm, PROJ_A_COLS // tn),
        in_specs=[
            pl.BlockSpec((tm, D_MODEL), lambda i, j: (i, 0)),
            pl.BlockSpec((None, 3, mod_rows, D_MODEL), lambda i, j: (l, 1, mod_row_block, 0)),
            pl.BlockSpec((None, tn, D_MODEL), lambda i, j: (l, j, 0)),
            pl.BlockSpec((None, 1, tn), lambda i, j: (l, 0, j)),
        ],
        out_specs=pl.BlockSpec((tm, tn), lambda i, j: (i, j)),
        out_shape=jax.ShapeDtypeStruct((m_tok, PROJ_A_COLS), F32),
        scratch_shapes=[pltpu.VMEM((tm, D_MODEL), BF16)],
        compiler_params=_params(("arbitrary", "arbitrary")),
        name="proj_a",
    )(x, mods, wt_a, bias_a)


def _rope_rows(res, cos, sin, row0):
    x1 = res[row0:row0 + ROT_HALF, :]
    x2 = res[row0 + ROT_HALF:row0 + ROT_DIM, :]
    return x1 * cos - x2 * sin, x2 * cos + x1 * sin


def _proj_b_body(x_ref, m_ref, w_ref, b_ref, cos_ref, sin_ref, o_ref, h_scr,
                 *, per_row, tiles_per_b, reps, n_sub, tf):
    i = pl.program_id(0)
    j = pl.program_id(1)

    @pl.when(j == 0)
    def _():
        b_idx = i // tiles_per_b
        shift = _mod_rows(m_ref, 0, per_row, b_idx, reps)
        scale = _mod_rows(m_ref, 1, per_row, b_idx, reps)
        h_scr[...] = _bf(x_ref[...] * (1.0 + scale) + shift)

    res = _dot_nt(_bf(w_ref[...]), h_scr[...]) + b_ref[...]
    sub = res.shape[1] // n_sub

    def store(rows, val):
        if n_sub == 1:
            o_ref[rows, :] = val
        else:
            for s in range(n_sub):
                o_ref[s, rows, :] = val[:, s * sub:(s + 1) * sub]

    store(slice(0, tf), res)

    @pl.when(j > 0)
    def _():
        cos = cos_ref[...]
        sin = sin_ref[...]
        for hh in range(N_KV):
            r0 = hh * HEAD_DIM
            n1, n2 = _rope_rows(res, cos, sin, r0)
            store(slice(r0, r0 + ROT_HALF), n1)
            store(slice(r0 + ROT_HALF, r0 + ROT_DIM), n2)


def _proj_b(x, mods, l, wt_full, bias_b, cos_t, sin_t, *, per_row, tm, rows_per_b, mod_row_block,
            mod_rows, n_sub):
    m_tok = x.shape[0]
    tf = KV_ROWS
    n_f = KV_W // tf
    row_block0 = (3 * D_MODEL + ATTN_W) // tf
    reps = tm // mod_rows if per_row else 1
    tiles_per_b = max(rows_per_b // tm, 1)
    body = functools.partial(_proj_b_body, per_row=per_row, tiles_per_b=tiles_per_b, reps=reps,
                             n_sub=n_sub, tf=tf)
    if n_sub == 1:
        n_b = m_tok // rows_per_b
        out_shape = jax.ShapeDtypeStruct((n_b, KV_W, rows_per_b), F32)
        out_spec = pl.BlockSpec((None, tf, tm), lambda i, j: (i // tiles_per_b, j, i % tiles_per_b))
        cs_spec = pl.BlockSpec((ROT_HALF, tm), lambda i, j: (0, i % tiles_per_b))
    else:
        sub = m_tok // n_sub
        out_shape = jax.ShapeDtypeStruct((n_sub, KV_W, sub), F32)
        out_spec = pl.BlockSpec((n_sub, tf, sub), lambda i, j: (0, j, 0))
        cs_spec = pl.BlockSpec((ROT_HALF, tm), lambda i, j: (0, 0))
    return pl.pallas_call(
        body,
        grid=(m_tok // tm, n_f),
        in_specs=[
            pl.BlockSpec((tm, D_MODEL), lambda i, j: (i, 0)),
            pl.BlockSpec((None, 3, mod_rows, D_MODEL), lambda i, j: (l, 1, mod_row_block, 0)),
            pl.BlockSpec((None, tf, D_MODEL), lambda i, j: (l, row_block0 + j, 0)),
            pl.BlockSpec((None, tf, 1), lambda i, j: (l, j, 0)),
            cs_spec,
            cs_spec,
        ],
        out_specs=out_spec,
        out_shape=out_shape,
        scratch_shapes=[pltpu.VMEM((tm, D_MODEL), BF16)],
        compiler_params=_params(("arbitrary", "arbitrary")),
        name="proj_b",
    )(x, mods, wt_full, bias_b, cos_t, sin_t)


def _proj_q_body(x_ref, m_ref, w_ref, b_ref, c_ref, slo_ref, shi_ref, o_ref, *, per_row, tiles_per_b, reps):
    i = pl.program_id(0)
    b_idx = i // tiles_per_b
    shift = _mod_rows(m_ref, 0, per_row, b_idx, reps)
    scale = _mod_rows(m_ref, 1, per_row, b_idx, reps)
    h = _bf(x_ref[...] * (1.0 + scale) + shift)
    q = _dot_nt(h, _bf(w_ref[...])) + b_ref[...]
    width = q.shape[1]
    n_rep = width // LANE
    cos = jnp.concatenate([c_ref[...]] * n_rep, axis=1)
    s_lo = jnp.concatenate([slo_ref[...]] * n_rep, axis=1)
    s_hi = jnp.concatenate([shi_ref[...]] * n_rep, axis=1)
    o_ref[...] = (q * cos + pltpu.roll(q, ROT_HALF, axis=1) * s_lo
                  + pltpu.roll(q, width - ROT_HALF, axis=1) * s_hi)


def _proj_q(x, mods, l, wt_full, bias_q, rope_tabs, *, per_row, tm, rows_per_b, mod_row_block, mod_rows):
    m_tok = x.shape[0]
    reps = tm // mod_rows if per_row else 1
    tiles_per_b = max(rows_per_b // tm, 1)
    row_block0 = (3 * D_MODEL) // ATTN_W
    body = functools.partial(_proj_q_body, per_row=per_row, tiles_per_b=tiles_per_b, reps=reps)
    if per_row:
        tab_spec = pl.BlockSpec((tm, LANE), lambda i: (0, 0))
    else:
        tab_spec = pl.BlockSpec((tm, LANE), lambda i: (i % tiles_per_b, 0))
    return pl.pallas_call(
        body,
        grid=(m_tok // tm,),
        in_specs=[
            pl.BlockSpec((tm, D_MODEL), lambda i: (i, 0)),
            pl.BlockSpec((None, 3, mod_rows, D_MODEL), lambda i: (l, 1, mod_row_block, 0)),
            pl.BlockSpec((None, ATTN_W, D_MODEL), lambda i: (l, row_block0, 0)),
            pl.BlockSpec((None, 1, ATTN_W), lambda i: (l, 0, 0)),
            tab_spec, tab_spec, tab_spec,
        ],
        out_specs=pl.BlockSpec((tm, ATTN_W), lambda i: (i, 0)),
        out_shape=jax.ShapeDtypeStruct((m_tok, ATTN_W), F32),
        compiler_params=_params(("arbitrary",)),
        name="proj_q",
    )(x, mods, wt_full, bias_q, *rope_tabs)


def _rglru_gates(xc, wg_ref, gb_ref, lam_ref):
    nblk = D_RNN // 256
    r_parts, i_parts = [], []
    for blk in range(nblk):
        xb = _bf(xc[:, blk * 256:(blk + 1) * 256])
        r_parts.append(_dot(xb, _bf(wg_ref[0, blk])))
        i_parts.append(_dot(xb, _bf(wg_ref[1, blk])))
    r = _sigmoid_t(jnp.concatenate(r_parts, axis=1) + gb_ref[0:1, :])
    ig = _sigmoid_t(jnp.concatenate(i_parts, axis=1) + gb_ref[1:2, :])
    lam = lam_ref[...]
    softplus_neg = jnp.maximum(-lam, 0.0) + jnp.log(1.0 + jnp.exp(-jnp.abs(lam)))
    log_a = -RG_C * r * softplus_neg
    a = jnp.exp(log_a)
    b = jnp.sqrt(1.0 - a * a) * (ig * xc)
    return a, b


def _rglru_prompt_body(u_ref, g_ref, cw_ref, cb_ref, wg_ref, gb_ref, lam_ref,
                       y_ref, hl_ref, ul_ref, h_scr, c_scr, *, tt):
    t = pl.program_id(1)

    @pl.when(t == 0)
    def _():
        h_scr[...] = jnp.zeros_like(h_scr)
        c_scr[...] = jnp.zeros_like(c_scr)

    u = u_ref[...]
    ext = jnp.concatenate([c_scr[...], u], axis=0)
    xc = cb_ref[...] + u * cw_ref[CONV_W - 1:CONV_W, :]
    for back in range(1, CONV_W):
        sh = pltpu.roll(ext, back, axis=0)[8:8 + tt]
        xc = xc + sh * cw_ref[CONV_W - 1 - back:CONV_W - back, :]
    a, b = _rglru_gates(xc, wg_ref, gb_ref, lam_ref)

    sub_row = lax.broadcasted_iota(jnp.int32, (tt, 1), 0) % 8
    d = 1
    while d < 8:
        a_sh = pltpu.roll(a, d, axis=0)
        b_sh = pltpu.roll(b, d, axis=0)
        valid = sub_row >= d
        b = jnp.where(valid, a * b_sh + b, b)
        a = jnp.where(valid, a * a_sh, a)
        d *= 2
    h_prev = h_scr[0:1, :]
    h_groups = []
    for k in range(tt // 8):
        h_k = b[8 * k:8 * k + 8, :] + a[8 * k:8 * k + 8, :] * h_prev
        h_groups.append(h_k)
        h_prev = h_k[7:8, :]
    h = jnp.concatenate(h_groups, axis=0)
    y_ref[...] = h * _gelu(g_ref[...])
    h_scr[...] = jnp.broadcast_to(h[tt - 1:tt, :], h_scr.shape)
    c_scr[...] = u[tt - 8:tt, :]

    @pl.when(t == pl.num_programs(1) - 1)
    def _():
        hl_ref[...] = h[tt - 1:tt, :]
        ul_ref[...] = u[tt - 8:tt, :]


def _rglru_prompt(proj_a, l, n_b, seq, conv_w, conv_b, wg, gate_b, lam):
    tt = 256
    nt = seq // tt
    body = functools.partial(_rglru_prompt_body, tt=tt)
    return pl.pallas_call(
        body,
        grid=(n_b, nt),
        in_specs=[
            pl.BlockSpec((tt, D_RNN), lambda b, t: (b * nt + t, 0)),
            pl.BlockSpec((tt, D_RNN), lambda b, t: (b * nt + t, 1)),
            pl.BlockSpec((None, CONV_W, D_RNN), lambda b, t: (l, 0, 0)),
            pl.BlockSpec((None, 1, D_RNN), lambda b, t: (l, 0, 0)),
            pl.BlockSpec((None, 2, D_RNN // 256, 256, 256), lambda b, t: (l, 0, 0, 0, 0)),
            pl.BlockSpec((None, 2, D_RNN), lambda b, t: (l, 0, 0)),
            pl.BlockSpec((None, 1, D_RNN), lambda b, t: (l, 0, 0)),
        ],
        out_specs=[
            pl.BlockSpec((tt, D_RNN), lambda b, t: (b * nt + t, 0)),
            pl.BlockSpec((None, 1, D_RNN), lambda b, t: (b, 0, 0)),
            pl.BlockSpec((None, 8, D_RNN), lambda b, t: (b, 0, 0)),
        ],
        out_shape=[
            jax.ShapeDtypeStruct((n_b * seq, D_RNN), F32),
            jax.ShapeDtypeStruct((n_b, 1, D_RNN), F32),
            jax.ShapeDtypeStruct((n_b, 8, D_RNN), F32),
        ],
        scratch_shapes=[pltpu.VMEM((8, D_RNN), F32), pltpu.VMEM((8, D_RNN), F32)],
        compiler_params=_params(("arbitrary", "arbitrary")),
        name="rglru_prompt",
    )(proj_a, proj_a, conv_w, conv_b, wg, gate_b, lam)


def _rglru_sample_body(u_ref, g_ref, buf_ref, h0_ref, cw_ref, cb_ref, wg_ref, gb_ref, lam_ref,
                       y_ref, hn_ref, cn_ref, *, n_b, n_s):
    u = u_ref[...]
    slabs = [buf_ref[r] for r in range(CONV_W - 1)] + [u[s * n_b:(s + 1) * n_b] for s in range(n_s)]
    xcs = []
    for s in range(n_s):
        xc = cb_ref[...] + slabs[s] * cw_ref[0:1, :]
        for k in range(1, CONV_W):
            xc = xc + slabs[s + k] * cw_ref[k:k + 1, :]
        xcs.append(xc)
    xc = jnp.concatenate(xcs, axis=0)
    a, b = _rglru_gates(xc, wg_ref, gb_ref, lam_ref)
    h = h0_ref[...]
    hs = []
    for s in range(n_s):
        h = a[s * n_b:(s + 1) * n_b] * h + b[s * n_b:(s + 1) * n_b]
        hs.append(h)
    y_ref[...] = jnp.concatenate(hs, axis=0) * _gelu(g_ref[...])
    hn_ref[...] = h
    for r in range(CONV_W - 1):
        cn_ref[r] = slabs[n_s + r]


def _rglru_sample(proj_a, l, n_b, n_s, conv_buf, h0, conv_w, conv_b, wg, gate_b, lam):
    m_tok = n_b * n_s
    body = functools.partial(_rglru_sample_body, n_b=n_b, n_s=n_s)
    return pl.pallas_call(
        body,
        grid=(1,),
        in_specs=[
            pl.BlockSpec((m_tok, D_RNN), lambda i: (0, 0)),
            pl.BlockSpec((m_tok, D_RNN), lambda i: (0, 1)),
            pl.BlockSpec((None, CONV_W - 1, n_b, D_RNN), lambda i: (l, 0, 0, 0)),
            pl.BlockSpec((None, n_b, D_RNN), lambda i: (l, 0, 0)),
            pl.BlockSpec((None, CONV_W, D_RNN), lambda i: (l, 0, 0)),
            pl.BlockSpec((None, 1, D_RNN), lambda i: (l, 0, 0)),
            pl.BlockSpec((None, 2, D_RNN // 256, 256, 256), lambda i: (l, 0, 0, 0, 0)),
            pl.BlockSpec((None, 2, D_RNN), lambda i: (l, 0, 0)),
            pl.BlockSpec((None, 1, D_RNN), lambda i: (l, 0, 0)),
        ],
        out_specs=[
            pl.BlockSpec((m_tok, D_RNN), lambda i: (0, 0)),
            pl.BlockSpec((n_b, D_RNN), lambda i: (0, 0)),
            pl.BlockSpec((CONV_W - 1, n_b, D_RNN), lambda i: (0, 0, 0)),
        ],
        out_shape=[
            jax.ShapeDtypeStruct((m_tok, D_RNN), F32),
            jax.ShapeDtypeStruct((n_b, D_RNN), F32),
            jax.ShapeDtypeStruct((CONV_W - 1, n_b, D_RNN), F32),
        ],
        compiler_params=_params(("arbitrary",)),
        name="rglru_sample",
    )(proj_a, proj_a, conv_buf, h0, conv_w, conv_b, wg, gate_b, lam)


def _pool_prompt_body(u_ref, w_ref, sc_ref, y_ref, last_ref, halo_scr, *, tt):
    t = pl.program_id(1)

    @pl.when(t == 0)
    def _():
        halo_scr[...] = jnp.zeros_like(halo_scr)

    x = u_ref[...]
    ext = jnp.concatenate([halo_scr[...], x], axis=0)
    pos = t * tt + lax.broadcasted_iota(jnp.int32, (tt, 1), 0)
    outs = []
    for gi, w in enumerate(POOL_WINDOWS):
        cols = slice(gi * POOL_GW, (gi + 1) * POOL_GW)
        s = ext[:, cols]
        k = 1
        while k < w:
            s = s + pltpu.roll(s, k, axis=0)
            k *= 2
        cnt = jnp.minimum(pos + 1, w).astype(F32)
        d = s[16:16 + tt] / cnt - x[:, cols]
        outs.append(_dot(_bf(d), _bf(w_ref[gi])))
    y_ref[...] = jnp.concatenate(outs, axis=1) * sc_ref[...]
    halo_scr[...] = x[tt - 16:tt, :]

    @pl.when(t == pl.num_programs(1) - 1)
    def _():
        last_ref[...] = x[tt - 16:tt, :]


def _pool_prompt(proj_a, l, n_b, seq, pool_w, pool_scale):
    tt = 256
    nt = seq // tt
    body = functools.partial(_pool_prompt_body, tt=tt)
    return pl.pallas_call(
        body,
        grid=(n_b, nt),
        in_specs=[
            pl.BlockSpec((tt, D_POOL), lambda b, t: (b * nt + t, 2)),
            pl.BlockSpec((None, len(POOL_WINDOWS), POOL_GW, POOL_GW), lambda b, t: (l, 0, 0, 0)),
            pl.BlockSpec((None, 1, D_POOL), lambda b, t: (l, 0, 0)),
        ],
        out_specs=[
            pl.BlockSpec((tt, D_POOL), lambda b, t: (b * nt + t, 0)),
            pl.BlockSpec((None, 16, D_POOL), lambda b, t: (b, 0, 0)),
        ],
        out_shape=[
            jax.ShapeDtypeStruct((n_b * seq, D_POOL), F32),
            jax.ShapeDtypeStruct((n_b, 16, D_POOL), F32),
        ],
        scratch_shapes=[pltpu.VMEM((16, D_POOL), F32)],
        compiler_params=_params(("arbitrary", "arbitrary")),
        name="pool_prompt",
    )(proj_a, pool_w, pool_scale)


def _pool_sample_body(u_ref, buf_ref, w_ref, sc_ref, y_ref, pn_ref, *, n_b, n_s):
    u = u_ref[...]
    slabs = [buf_ref[r] for r in range(POOL_BUF)] + [u[s * n_b:(s + 1) * n_b] for s in range(n_s)]
    outs = []
    for gi, w in enumerate(POOL_WINDOWS):
        cols = slice(gi * POOL_GW, (gi + 1) * POOL_GW)
        ds = []
        for s in range(n_s):
            acc = slabs[POOL_BUF + s][:, cols]
            for back in range(1, w):
                acc = acc + slabs[POOL_BUF + s - back][:, cols]
            ds.append(acc / float(w) - slabs[POOL_BUF + s][:, cols])
        outs.append(_dot(_bf(jnp.concatenate(ds, axis=0)), _bf(w_ref[gi])))
    y_ref[...] = jnp.concatenate(outs, axis=1) * sc_ref[...]
    for r in range(POOL_BUF):
        pn_ref[r] = slabs[n_s + r]


def _pool_sample(proj_a, l, n_b, n_s, pool_buf, pool_w, pool_scale):
    m_tok = n_b * n_s
    body = functools.partial(_pool_sample_body, n_b=n_b, n_s=n_s)
    return pl.pallas_call(
        body,
        grid=(1,),
        in_specs=[
            pl.BlockSpec((m_tok, D_POOL), lambda i: (0, 2)),
            pl.BlockSpec((None, POOL_BUF, n_b, D_POOL), lambda i: (l, 0, 0, 0)),
            pl.BlockSpec((None, len(POOL_WINDOWS), POOL_GW, POOL_GW), lambda i: (l, 0, 0, 0)),
            pl.BlockSpec((None, 1, D_POOL), lambda i: (l, 0, 0)),
        ],
        out_specs=[
            pl.BlockSpec((m_tok, D_POOL), lambda i: (0, 0)),
            pl.BlockSpec((POOL_BUF, n_b, D_POOL), lambda i: (0, 0, 0)),
        ],
        out_shape=[
            jax.ShapeDtypeStruct((m_tok, D_POOL), F32),
            jax.ShapeDtypeStruct((POOL_BUF, n_b, D_POOL), F32),
        ],
        compiler_params=_params(("arbitrary",)),
        name="pool_sample",
    )(proj_a, pool_buf, pool_w, pool_scale)


def _pe_term_body(pe_ref, w1_ref, b1_ref, o_ref):
    pe = jnp.broadcast_to(pe_ref[...], (8, pe_ref.shape[1]))
    o_ref[...] = _dot(_bf(pe), _bf(w1_ref[...]))[0:1, :] + b1_ref[...]


def _pe_term(cmp_pe, cmp_w1, cmp_b1):
    depth = cmp_pe.shape[0]
    flat = CMP_BLOCK * HEAD_DIM
    pe4 = cmp_pe.reshape(depth, 2, 1, flat)
    b4 = cmp_b1.reshape(depth, 2, 1, CMP_HIDDEN)
    return pl.pallas_call(
        _pe_term_body,
        grid=(depth, 2),
        in_specs=[
            pl.BlockSpec((None, None, 1, flat), lambda l, c: (l, c, 0, 0)),
            pl.BlockSpec((None, None, flat, CMP_HIDDEN), lambda l, c: (l, c, 0, 0)),
            pl.BlockSpec((None, None, 1, CMP_HIDDEN), lambda l, c: (l, c, 0, 0)),
        ],
        out_specs=pl.BlockSpec((None, None, 1, CMP_HIDDEN), lambda l, c: (l, c, 0, 0)),
        out_shape=jax.ShapeDtypeStruct((depth, 2, 1, CMP_HIDDEN), F32),
        compiler_params=_params(("arbitrary", "arbitrary")),
        name="pe_term",
    )(pe4, cmp_w1, b4)


def _compress_body(*refs, n_pages, page):
    pt_ref = refs[0]
    page_refs = refs[1:1 + n_pages]
    w1_ref, pet_ref, w2t_ref, o_ref, s_scr, x_scr = refs[1 + n_pages:]
    del pt_ref
    n_chunks = n_pages * page // CMP_STRIDE
    n_slab = KV_ROWS // LANE
    for p in range(n_pages):
        for j in range(n_slab):
            blk = page_refs[p][j * LANE:(j + 1) * LANE, :]
            s_scr[j, p * page:(p + 1) * page, :] = blk.T
    lane = lax.broadcasted_iota(jnp.int32, (n_chunks, LANE), 1)
    low = lane < HEAD_DIM
    for c in range(2):
        for gp in range(2):
            j = 2 * c + gp
            for r2 in range(CMP_STRIDE // 2):
                v0 = s_scr[j, pl.ds(2 * r2, n_chunks, stride=CMP_STRIDE), :]
                v1 = s_scr[j, pl.ds(2 * r2 + 1, n_chunks, stride=CMP_STRIDE), :]
                xe = jnp.where(low, v0, pltpu.roll(v1, HEAD_DIM, axis=1))
                xo = jnp.where(low, pltpu.roll(v0, HEAD_DIM, axis=1), v1)
                x_scr[(2 * gp) * n_chunks:(2 * gp + 1) * n_chunks, r2 * LANE:(r2 + 1) * LANE] = _bf(xe)
                x_scr[(2 * gp + 1) * n_chunks:(2 * gp + 2) * n_chunks, r2 * LANE:(r2 + 1) * LANE] = _bf(xo)
        pm = _dot(x_scr[...], _bf(w1_ref[c]))
        rows = pm.shape[0]
        pre = pm[:, :CMP_HIDDEN] + pltpu.roll(pm[:, CMP_HIDDEN:], rows - 1, axis=0)
        hid = _gelu(pre + pet_ref[c])
        out_t = _dot_nt(_bf(w2t_ref[c]), _bf(hid))
        for g in range(N_KV):
            o_ref[c, g * HEAD_DIM:(g + 1) * HEAD_DIM, :] = out_t[:, g * n_chunks:(g + 1) * n_chunks]


def _compress(page_table, src, page_specs, n_seq, l, w1c, pe_term, w2t):
    n_pages = len(page_specs)
    page = LANE
    n_chunks = n_pages * page // CMP_STRIDE
    body = functools.partial(_compress_body, n_pages=n_pages, page=page)
    grid_spec = pltpu.PrefetchScalarGridSpec(
        num_scalar_prefetch=1,
        grid=(n_seq,),
        in_specs=list(page_specs) + [
            pl.BlockSpec((None, 2, CMP_STRIDE * HEAD_DIM, 2 * CMP_HIDDEN), lambda b, pt: (l, 0, 0, 0)),
            pl.BlockSpec((None, 2, 1, CMP_HIDDEN), lambda b, pt: (l, 0, 0, 0)),
            pl.BlockSpec((None, 2, HEAD_DIM, CMP_HIDDEN), lambda b, pt: (l, 0, 0, 0)),
        ],
        out_specs=pl.BlockSpec((None, 2, K_ROWS, n_chunks), lambda b, pt: (b, 0, 0, 0)),
        scratch_shapes=[
            pltpu.VMEM((KV_ROWS // LANE, n_pages * page, LANE), F32),
            pltpu.VMEM((N_KV * n_chunks, CMP_STRIDE * HEAD_DIM), BF16),
        ],
    )
    return pl.pallas_call(
        body,
        grid_spec=grid_spec,
        out_shape=jax.ShapeDtypeStruct((n_seq, 2, K_ROWS, n_chunks), F32),
        compiler_params=_params(("arbitrary",)),
        name="compress",
    )(page_table, *([src] * n_pages), w1c, pe_term, w2t)


def _select_blocks(score_t, cur, n_blocks):
    rows = score_t.shape[0]
    j = lax.broadcasted_iota(jnp.int32, score_t.shape, 0)
    forced = (j == 0) | (j == cur) | (j == cur - 1)
    sc = jnp.where(j > cur, -jnp.inf, jnp.where(forced, jnp.inf, score_t))
    rank = jnp.zeros(score_t.shape, jnp.int32)
    for jp in range(n_blocks):
        row = sc[jp:jp + 1, :]
        ahead = (row > sc) | ((row == sc) & (jp < j))
        rank = rank + ahead.astype(jnp.int32)
    del rows
    return ((rank < N_SELECT) & (j <= cur)).astype(F32)


def _square_t(x):
    ra, cb = x.shape[0] // LANE, x.shape[1] // LANE
    rows = []
    for c in range(cb):
        rows.append(jnp.concatenate([x[r * LANE:(r + 1) * LANE, c * LANE:(c + 1) * LANE].T for r in range(ra)],
                                    axis=1) if ra > 1 else x[0:LANE, c * LANE:(c + 1) * LANE].T)
    return jnp.concatenate(rows, axis=0) if cb > 1 else rows[0]


def _rows_to_key_major(k_t):
    padded = jnp.concatenate([k_t, jnp.zeros((LANE - HEAD_DIM, k_t.shape[1]), F32)], axis=0)
    return _bf(padded.T)


def _attn_prompt_km_body(q_ref, sk_ref, sv_ref, wk_ref, wv_ref, kc_ref, vc_ref, gate_ref,
                         o_ref, ks_scr, kw_scr, imp_scr, sel_scr, g_scr, os_scr, *, tq, ck, seq):
    g = pl.program_id(1)
    i = pl.program_id(2)
    q0 = i * tq
    hp = HEADS_PER_KV
    cols = hp * tq

    @pl.when(i == 0)
    def _():
        for t in range(seq // LANE):
            ks_scr[t * LANE:(t + 1) * LANE, :] = _rows_to_key_major(sk_ref[:, t * LANE:(t + 1) * LANE])
            kw_scr[t * LANE:(t + 1) * LANE, :] = _rows_to_key_major(wk_ref[:, t * LANE:(t + 1) * LANE])

    q_t = _square_t(q_ref[...] * ATTN_SCALE)
    q_cols = jnp.concatenate([q_t[h * HEAD_DIM:(h + 1) * HEAD_DIM, :] for h in range(hp)], axis=1)
    q_cols = _bf(jnp.concatenate([q_cols, jnp.zeros((LANE - HEAD_DIM, cols), F32)], axis=0))
    qpos = q0 + lax.broadcasted_iota(jnp.int32, (1, cols), 1) % tq
    qpos_q = q0 + lax.broadcasted_iota(jnp.int32, (1, tq), 1)

    n_blk = kc_ref.shape[1]
    s_c = _dot(_rows_to_key_major(kc_ref[...]), q_cols)
    blk_end = lax.broadcasted_iota(jnp.int32, (n_blk, 1), 0) * CMP_STRIDE + (CMP_BLOCK - 1)
    cmask = blk_end <= qpos
    s_c = jnp.where(cmask, s_c, NEG_BIG)
    m_c = jnp.max(s_c, axis=0, keepdims=True)
    e_c = jnp.where(cmask, jnp.exp(s_c - m_c), 0.0)
    p_c = e_c / jnp.maximum(jnp.sum(e_c, axis=0, keepdims=True), 1e-30)
    o_c = _dot(_bf(vc_ref[...]), _bf(p_c))

    imp_t = p_c[:, 0:tq]
    for h in range(1, hp):
        imp_t = imp_t + p_c[:, h * tq:(h + 1) * tq]
    imp_scr[0:8, :] = jnp.zeros((8, tq), F32)
    imp_scr[8:8 + n_blk, :] = imp_t
    n_sel_blocks = seq // SLC_BLOCK
    ratio = SLC_BLOCK // CMP_STRIDE
    score_t = imp_scr[pl.ds(7, n_sel_blocks, stride=ratio), :]
    for o in range(1, ratio + 1):
        score_t = score_t + imp_scr[pl.ds(7 + o, n_sel_blocks, stride=ratio), :]
    sel_scr[...] = _select_blocks(score_t, qpos_q // SLC_BLOCK, n_sel_blocks)

    sub = lax.broadcasted_iota(jnp.int32, (SLC_BLOCK, 1), 0)

    def sel_branch(n_keys):
        parts = []
        for j in range(n_keys // SLC_BLOCK):
            ok = (sel_scr[j:j + 1, :] > 0.5) & (j * SLC_BLOCK + sub <= qpos_q)
            parts.append(jnp.where(ok, 0.0, NEG_BIG))
        bias = jnp.concatenate(parts, axis=0)
        s = _dot(ks_scr[0:n_keys, :], q_cols) + jnp.concatenate([bias] * hp, axis=1)
        p = jnp.exp(s - jnp.max(s, axis=0, keepdims=True))
        acc = _dot(_bf(sv_ref[:, 0:n_keys]), _bf(p))
        os_scr[...] = acc / jnp.sum(p, axis=0, keepdims=True)

    n_need = (q0 + tq + ck - 1) // ck
    for n in range(1, seq // ck + 1):
        pl.when(n_need == n)(functools.partial(sel_branch, n * ck))
    o_s = os_scr[...]

    span = WINDOW + tq
    start = pl.multiple_of(jnp.maximum(q0 - WINDOW, 0), LANE)
    dpos = qpos - (start + lax.broadcasted_iota(jnp.int32, (span, 1), 0))
    w_ok = (dpos >= 0) & (dpos < WINDOW)
    s_w = jnp.where(w_ok, _dot(kw_scr[pl.ds(start, span), :], q_cols), NEG_BIG)
    p_w = jnp.exp(s_w - jnp.max(s_w, axis=0, keepdims=True))
    o_w = _dot(_bf(wv_ref[:, pl.ds(start, span)]), _bf(p_w)) / jnp.sum(p_w, axis=0, keepdims=True)

    g_scr[...] = _sigmoid(gate_ref[...]).T
    heads = []
    for h in range(hp):
        cs = slice(h * tq, (h + 1) * tq)
        base = (g * hp + h) * 3
        acc = g_scr[pl.ds(base, 1), :] * o_c[:, cs]
        acc = acc + g_scr[pl.ds(base + 1, 1), :] * o_s[:, cs]
        acc = acc + g_scr[pl.ds(base + 2, 1), :] * o_w[:, cs]
        heads.append(acc)
    o_ref[...] = _square_t(jnp.concatenate(heads, axis=0))


def _attn_prompt_km(q_tok, kv_t, c_t, proj_a, n_b, seq):
    tq = LANE
    ck = 512
    assert seq % ck == 0 and seq >= WINDOW + tq
    nq = seq // tq
    q_w = HEADS_PER_KV * HEAD_DIM
    slc0 = KV_ROWS // HEAD_DIM
    win0 = 2 * KV_ROWS // HEAD_DIM
    body = functools.partial(_attn_prompt_km_body, tq=tq, ck=ck, seq=seq)
    n_blk = c_t.shape[-1]
    return pl.pallas_call(
        body,
        grid=(n_b, N_KV, nq),
        in_specs=[
            pl.BlockSpec((tq, q_w), lambda b, g, i: (b * nq + i, g)),
            pl.BlockSpec((None, HEAD_DIM, seq), lambda b, g, i: (b, slc0 + g, 0)),
            pl.BlockSpec((None, HEAD_DIM, seq), lambda b, g, i: (b, slc0 + N_KV + g, 0)),
            pl.BlockSpec((None, HEAD_DIM, seq), lambda b, g, i: (b, win0 + g, 0)),
            pl.BlockSpec((None, HEAD_DIM, seq), lambda b, g, i: (b, win0 + N_KV + g, 0)),
            pl.BlockSpec((None, None, HEAD_DIM, n_blk), lambda b, g, i: (b, 0, g, 0)),
            pl.BlockSpec((None, None, HEAD_DIM, n_blk), lambda b, g, i: (b, 1, g, 0)),
            pl.BlockSpec((tq, LANE), lambda b, g, i: (b * nq + i, GATE_COL_BLOCK)),
        ],
        out_specs=pl.BlockSpec((tq, q_w), lambda b, g, i: (b * nq + i, g)),
        out_shape=jax.ShapeDtypeStruct((n_b * seq, ATTN_W), F32),
        scratch_shapes=[
            pltpu.VMEM((seq, LANE), BF16),
            pltpu.VMEM((seq, LANE), BF16),
            pltpu.VMEM((8 + n_blk + 8, tq), F32),
            pltpu.VMEM((seq // SLC_BLOCK, tq), F32),
            pltpu.VMEM((LANE, tq), F32),
            pltpu.VMEM((HEAD_DIM, HEADS_PER_KV * tq), F32),
        ],
        compiler_params=_params(("arbitrary", "arbitrary", "arbitrary")),
        name="attn_prompt",
    )(q_tok, kv_t, kv_t, kv_t, kv_t, c_t, c_t, proj_a)


def _attn_sample_body(*refs, n_pages, page, n_s, past):
    pt_ref = refs[0]
    page_refs = refs[1:1 + n_pages]
    q_ref, win_ref, new_ref, c_ref, gate_ref, e_ref, o_ref, s_scr, imp_scr = refs[1 + n_pages:]
    del pt_ref
    hp = HEADS_PER_KV
    rpg = n_s * hp
    rows = N_KV * rpg
    n_keys = n_pages * page
    row_i = lax.broadcasted_iota(jnp.int32, (rows, 1), 0)
    row_g = row_i // rpg
    row_s = (row_i % rpg) // hp
    col_g = lax.broadcasted_iota(jnp.int32, (1, K_ROWS), 1) // HEAD_DIM
    q2 = q_ref[...] * ATTN_SCALE
    diag = row_g == col_g
    q_bd = _bf(jnp.where(diag, jnp.concatenate([q2] * (K_ROWS // LANE), axis=1), 0.0))

    def fold(o_full):
        z = jnp.where(diag, o_full, 0.0)
        w = z[:, 0:LANE]
        for blk in range(1, K_ROWS // LANE):
            w = w + z[:, blk * LANE:(blk + 1) * LANE]
        return w + pltpu.roll(w, HEAD_DIM, axis=1)

    n_blk = c_ref.shape[-1]
    blk_end = lax.broadcasted_iota(jnp.int32, (1, n_blk), 1) * CMP_STRIDE + (CMP_BLOCK - 1)
    cmask = blk_end <= past + row_s
    s_c = jnp.where(cmask, _dot(q_bd, _bf(c_ref[0])), NEG_BIG)
    m_c = jnp.max(s_c, axis=-1, keepdims=True)
    e_c = jnp.where(cmask, jnp.exp(s_c - m_c), 0.0)
    p_c = e_c / jnp.maximum(jnp.sum(e_c, axis=-1, keepdims=True), 1e-30)
    o_c = fold(_dot_nt(_bf(p_c), _bf(c_ref[1])))

    t1 = p_c + pltpu.roll(p_c, rows - 1, axis=0)
    imp = t1 + pltpu.roll(t1, rows - 2, axis=0)
    imp_pad = jnp.concatenate([imp, jnp.zeros((LANE - rows, n_blk), F32)], axis=0)
    imp_scr[...] = jnp.zeros(imp_scr.shape, F32)
    imp_scr[8:8 + n_blk, :] = imp_pad.T
    n_sel_rows = 40
    n_sel_blocks = (past + n_s + SLC_BLOCK - 1) // SLC_BLOCK
    ratio = SLC_BLOCK // CMP_STRIDE
    score_t = imp_scr[pl.ds(7, n_sel_rows, stride=ratio), :]
    for o in range(1, ratio + 1):
        score_t = score_t + imp_scr[pl.ds(7 + o, n_sel_rows, stride=ratio), :]
    lane_row = lax.broadcasted_iota(jnp.int32, (1, LANE), 1)
    cur = (past + (lane_row % rpg) // hp) // SLC_BLOCK
    sel_t = _select_blocks(score_t, cur, n_sel_blocks)
    sel_pad = jnp.concatenate([sel_t, jnp.zeros((LANE - n_sel_rows, LANE), F32)], axis=0)
    sel_rows = _bf(sel_pad.T)
    expand = (lax.broadcasted_iota(jnp.int32, (rows, LANE), 1) == (row_i // hp) * hp).astype(BF16)
    sel64 = _bf(_dot(expand, sel_rows))

    n_new = 16
    new_all = new_ref[...]
    new_sk = _bf(new_all[:, KV_ROWS:KV_ROWS + K_ROWS])
    new_sv = _bf(new_all[:, KV_ROWS + K_ROWS:2 * KV_ROWS])
    new_wk = _bf(new_all[:, 2 * KV_ROWS:2 * KV_ROWS + K_ROWS])
    new_wv = _bf(new_all[:, 2 * KV_ROWS + K_ROWS:3 * KV_ROWS])
    new_i = lax.broadcasted_iota(jnp.int32, (1, n_new), 1)
    new_ok = (new_i <= row_s) & (new_i < n_s)

    for p in range(n_pages):
        s_scr[:, p * page:(p + 1) * page] = _dot(q_bd, _bf(page_refs[p][0:K_ROWS, :]))
    picked = _dot(sel64, e_ref[...])
    s_all = jnp.where(picked > 0.5, s_scr[...], NEG_BIG)
    s_new = jnp.where(new_ok, _dot_nt(q_bd, new_sk), NEG_BIG)
    m_s = jnp.maximum(jnp.max(s_all, axis=-1, keepdims=True), jnp.max(s_new, axis=-1, keepdims=True))
    p_all = jnp.exp(s_all - m_s)
    p_new = jnp.exp(s_new - m_s)
    l_s = jnp.sum(p_all, axis=-1, keepdims=True) + jnp.sum(p_new, axis=-1, keepdims=True)
    acc = _dot(_bf(p_new), new_sv)
    p_all = _bf(p_all)
    for p in range(n_pages):
        acc = acc + _dot_nt(p_all[:, p * page:(p + 1) * page], _bf(page_refs[p][K_ROWS:KV_ROWS, :]))
    o_s = fold(acc) / l_s

    wlen = win_ref.shape[1]
    kk = lax.broadcasted_iota(jnp.int32, (1, wlen), 1)
    dpos = (past + row_s) - (past - wlen + kk)
    w_ok = (dpos >= 0) & (dpos < WINDOW)
    s_w = jnp.where(w_ok, _dot(q_bd, _bf(win_ref[0:K_ROWS, :])), NEG_BIG)
    s_wn = jnp.where(new_ok, _dot_nt(q_bd, new_wk), NEG_BIG)
    m_w = jnp.maximum(jnp.max(s_w, axis=-1, keepdims=True), jnp.max(s_wn, axis=-1, keepdims=True))
    p_w = jnp.exp(s_w - m_w)
    p_wn = jnp.exp(s_wn - m_w)
    l_w = jnp.sum(p_w, axis=-1, keepdims=True) + jnp.sum(p_wn, axis=-1, keepdims=True)
    acc_w = _dot_nt(_bf(p_w), _bf(win_ref[K_ROWS:KV_ROWS, :])) + _dot(_bf(p_wn), new_wv)
    o_w = fold(acc_w) / l_w

    gates = _sigmoid(gate_ref[...])
    o_ref[...] = gates[:, 0:1] * o_c + gates[:, 1:2] * o_s + gates[:, 2:3] * o_w


def _attn_sample(page_table, cache_pages, page_specs, q_s, win_state, new_kv, c_t, gates, e_mat, l,
                 n_b, n_s, past):
    n_pages = len(page_specs)
    page = LANE
    rows = N_KV * n_s * HEADS_PER_KV
    n_blk = c_t.shape[-1]
    wlen = win_state.shape[-1]
    body = functools.partial(_attn_sample_body, n_pages=n_pages, page=page, n_s=n_s, past=past)
    grid_spec = pltpu.PrefetchScalarGridSpec(
        num_scalar_prefetch=1,
        grid=(n_b,),
        in_specs=list(page_specs) + [
            pl.BlockSpec((None, rows, LANE), lambda b, pt: (b, 0, 0)),
            pl.BlockSpec((None, KV_ROWS, wlen), lambda b, pt: (l * n_b + b, 0, 0)),
            pl.BlockSpec((None, new_kv.shape[1], KV_W), lambda b, pt: (b, 0, 0)),
            pl.BlockSpec((None, 2, K_ROWS, n_blk), lambda b, pt: (b, 0, 0, 0)),
            pl.BlockSpec((None, rows, 3), lambda b, pt: (b, 0, 0)),
            pl.BlockSpec((LANE, n_pages * page), lambda b, pt: (0, 0)),
        ],
        out_specs=pl.BlockSpec((None, rows, LANE), lambda b, pt: (b, 0, 0)),
        scratch_shapes=[
            pltpu.VMEM((rows, n_pages * page), F32),
            pltpu.VMEM((176, LANE), F32),
        ],
    )
    return pl.pallas_call(
        body,
        grid_spec=grid_spec,
        out_shape=jax.ShapeDtypeStruct((n_b, rows, LANE), F32),
        compiler_params=_params(("arbitrary",)),
        name="attn_sample",
    )(page_table, *([cache_pages] * n_pages), q_s, win_state, new_kv, c_t, gates, e_mat)


def _merge_body(x_ref, m_ref, y0_ref, y1_ref, y2_ref, gm_ref, w_ref, g_ref, b_ref, o_ref, acc_scr,
                *, per_row, tiles_per_b, reps, alpha):
    i = pl.program_id(0)
    k = pl.program_id(1)

    def branch(y_ref, idx):
        contrib = _sigmoid(gm_ref[...]) * _dot(_bf(y_ref[...]), w_ref[idx])
        if idx == 0:
            acc_scr[...] = contrib
        else:
            acc_scr[...] += contrib

    @pl.when(k == 0)
    def _():
        branch(y0_ref, 0)

    @pl.when(k == 1)
    def _():
        branch(y1_ref, 1)

    @pl.when(k == 2)
    def _():
        branch(y2_ref, 2)

    @pl.when(k == 3)
    def _():
        out = _dot(_bf(acc_scr[...]), w_ref[3])
        gate = _mod_rows(m_ref, 2, per_row, i // tiles_per_b, reps)
        z = alpha * x_ref[...] + gate * out
        o_ref[...] = _layer_norm(z, g_ref[...], b_ref[...])


def _merge(x, mods, l, y_rnn, y_pool, y_attn, proj_a, w_stack, ln_g, ln_b, *, per_row, tm, rows_per_b,
           mod_row_block, mod_rows, alpha):
    m_tok = x.shape[0]
    reps = tm // mod_rows if per_row else 1
    tiles_per_b = max(rows_per_b // tm, 1)
    body = functools.partial(_merge_body, per_row=per_row, tiles_per_b=tiles_per_b, reps=reps, alpha=alpha)
    y_spec = pl.BlockSpec((tm, D_MODEL), lambda i, k: (i, 0))
    return pl.pallas_call(
        body,
        grid=(m_tok // tm, 4),
        in_specs=[
            y_spec,
            pl.BlockSpec((None, 3, mod_rows, D_MODEL), lambda i, k: (l, 1, mod_row_block, 0)),
            y_spec, y_spec, y_spec,
            pl.BlockSpec((tm, D_MODEL), lambda i, k: (i, 3 + jnp.minimum(k, 2))),
            pl.BlockSpec((None, 4, D_MODEL, D_MODEL), lambda i, k: (l, 0, 0, 0)),
            pl.BlockSpec((None, None, 1, D_MODEL), lambda i, k: (l, 1, 0, 0)),
            pl.BlockSpec((None, None, 1, D_MODEL), lambda i, k: (l, 1, 0, 0)),
        ],
        out_specs=y_spec,
        out_shape=jax.ShapeDtypeStruct((m_tok, D_MODEL), F32),
        scratch_shapes=[pltpu.VMEM((tm, D_MODEL), F32)],
        compiler_params=_params(("arbitrary", "arbitrary")),
        name="merge",
    )(x, mods, y_rnn, y_pool, y_attn, proj_a, w_stack, ln_g, ln_b)


def _rope_tables(pos):
    inv = ROPE_THETA ** (-jnp.arange(ROT_HALF, dtype=F32) * (2.0 / ROT_DIM))
    ang = inv[:, None] * pos.astype(F32)[None, :]
    return jnp.cos(ang), jnp.sin(ang)


def _rope_lane_tables(cos, sin):
    d = np.arange(LANE) % HEAD_DIM
    idx = np.where(d < ROT_HALF, d, np.clip(d - ROT_HALF, 0, ROT_HALF - 1))
    cos_l = cos.T[:, idx]
    sin_l = sin.T[:, idx]
    first = jnp.asarray(d < ROT_HALF)[None, :]
    second = jnp.asarray((d >= ROT_HALF) & (d < ROT_DIM))[None, :]
    c_tab = jnp.where(first | second, cos_l, 1.0)
    s_lo = jnp.where(second, sin_l, 0.0)
    s_hi = jnp.where(first, -sin_l, 0.0)
    return c_tab, s_lo, s_hi


def kernel(x_prompt, x_sample, c_prompt, c_sample, cache_cmp_kv, cache_slc_kv, state_win_kv, state_rnn_h, state_rnn_conv, state_pool, page_table, ada_w, ada_b, ln_g, ln_b, ffn_w_up, ffn_w_down, w_in, b_in, rnn_conv_w, rnn_conv_b, rnn_gate_w, rnn_gate_b, rnn_lambda, pool_w, pool_scale, cmp_w1, cmp_b1, cmp_w2, cmp_pe, w_rnn_out, w_pool_out, w_attn_out, w_out):
    depth = ada_w.shape[0]
    n_bp, seq, _ = x_prompt.shape
    n_bs, n_s, _ = x_sample.shape
    n_pool, page = cache_cmp_kv.shape[1], cache_cmp_kv.shape[2]
    n_pages = page_table.shape[1]
    past = n_pages * page
    wlen = state_win_kv.shape[2]
    alpha = (2 * depth) ** 0.25
    assert page == LANE and seq == n_pages * page and n_bs == LANE

    c_all = jnp.concatenate([c_sample, c_prompt], axis=0)
    mods = _adaln(c_all, ada_w, ada_b)
    mod_p = dict(per_row=False, rows_per_b=seq, mod_row_block=n_bs // n_bp, mod_rows=n_bp)
    mod_s = dict(per_row=True, rows_per_b=n_bs * n_s, mod_row_block=0, mod_rows=n_bs)
    ln_g4 = ln_g.reshape(depth, 3, 1, D_MODEL)
    ln_b4 = ln_b.reshape(depth, 3, 1, D_MODEL)
    wt_full = jnp.swapaxes(w_in, 1, 2)
    c0 = 3 * D_MODEL + ATTN_W + KV_W
    c1 = c0 + 3 * N_HEADS
    pad_cols = PROJ_A_COLS - 6 * D_MODEL - 3 * N_HEADS
    wt_a = jnp.concatenate([wt_full[:, :3 * D_MODEL], wt_full[:, c1:], wt_full[:, c0:c1],
                            jnp.zeros((depth, pad_cols, D_MODEL), F32)], axis=1)
    bias_a = jnp.concatenate([b_in[:, :3 * D_MODEL], b_in[:, c1:], b_in[:, c0:c1],
                              jnp.zeros((depth, pad_cols), F32)], axis=1).reshape(depth, 1, PROJ_A_COLS)
    q0c = 3 * D_MODEL
    bias_q = b_in[:, q0c:q0c + ATTN_W].reshape(depth, 1, ATTN_W)
    bias_b = b_in[:, q0c + ATTN_W:q0c + ATTN_W + KV_W].reshape(depth, KV_W, 1)
    conv_b3 = rnn_conv_b.reshape(depth, 1, D_RNN)
    lam3 = rnn_lambda.reshape(depth, 1, D_RNN)
    per = 256 // RNN_BW
    gw = rnn_gate_w.reshape(depth, 2, D_RNN // 256, per, RNN_BW, RNN_BW)
    eye = jnp.eye(per, dtype=F32)
    wg = jnp.einsum('lcbpij,pq->lcbpiqj', gw, eye).reshape(depth, 2, D_RNN // 256, 256, 256)
    pool_scale3 = pool_scale.reshape(depth, 1, D_POOL)
    half = CMP_STRIDE * HEAD_DIM
    w1c = jnp.concatenate([cmp_w1[:, :, :half, :], cmp_w1[:, :, half:, :]], axis=-1)
    w2t = jnp.swapaxes(cmp_w2, 2, 3)
    pe_term = _pe_term(cmp_pe, cmp_w1, cmp_b1)
    w_stack = _bf(jnp.stack([w_rnn_out, w_pool_out, w_attn_out, w_out], axis=1))
    e_mat = (jnp.arange(LANE, dtype=jnp.int32)[:, None]
             == (jnp.arange(seq, dtype=jnp.int32)[None, :] // SLC_BLOCK)).astype(BF16)
    cos_p, sin_p = _rope_tables(jnp.arange(seq, dtype=jnp.int32))
    pos_s = past + jnp.repeat(jnp.arange(n_s, dtype=jnp.int32), n_bs)
    cos_s, sin_s = _rope_tables(pos_s)
    qtab_p = _rope_lane_tables(cos_p, sin_p)
    qtab_s = _rope_lane_tables(cos_s, sin_s)

    def kv_t(a):
        lead = a.shape[:-4]
        nd = len(lead)
        a = jnp.transpose(a, tuple(range(nd)) + (nd + 1, nd + 2, nd + 3, nd))
        return a.reshape(lead + (KV_ROWS, a.shape[-1]))

    cache_cmp_t = kv_t(cache_cmp_kv).reshape(depth * n_pool, KV_ROWS, page)
    cache_slc_t = kv_t(cache_slc_kv).reshape(depth * n_pool, KV_ROWS, page)
    win_t = kv_t(state_win_kv)
    win_flat = win_t.reshape(depth * n_bs, KV_ROWS, wlen)
    conv_state = jnp.swapaxes(state_rnn_conv, 1, 2)
    pool_state = jnp.swapaxes(state_pool, 1, 2)
    page_ident = jnp.arange(n_bp * n_pages, dtype=jnp.int32).reshape(n_bp, n_pages) % n_pages

    xp = x_prompt.reshape(n_bp * seq, D_MODEL)
    xs = jnp.swapaxes(x_sample, 0, 1).reshape(n_s * n_bs, D_MODEL)
    tm_p = 1024
    tm_s = n_s * n_bs

    outs = {k: [] for k in ("cmp_p", "cmp_s", "slc_p", "slc_s", "win_p", "win_s", "h_p", "h_s",
                            "conv_p", "conv_s", "pool_p", "pool_s")}
    for l in range(depth):
        xp = _ffn(xp, mods, l, 0, ffn_w_up, ffn_w_down, ln_g4, ln_b4, 0, tm=tm_p, alpha=alpha, **mod_p)
        xs = _ffn(xs, mods, l, 0, ffn_w_up, ffn_w_down, ln_g4, ln_b4, 0, tm=tm_s, alpha=alpha, **mod_s)

        pa_p = _proj_a(xp, mods, l, wt_a, bias_a, tm=tm_p, **mod_p)
        pa_s = _proj_a(xs, mods, l, wt_a, bias_a, tm=tm_s, **mod_s)
        kv_p = _proj_b(xp, mods, l, wt_full, bias_b, cos_p, sin_p, tm=tm_p, n_sub=1, **mod_p)
        kv_s = _proj_b(xs, mods, l, wt_full, bias_b, cos_s, sin_s, tm=tm_s, n_sub=n_s, **mod_s)
        q_p = _proj_q(xp, mods, l, wt_full, bias_q, qtab_p, tm=tm_p, **mod_p)
        q_s = _proj_q(xs, mods, l, wt_full, bias_q, qtab_s, tm=tm_s, **mod_s)

        y_rnn_p, hl_p, ul_p = _rglru_prompt(pa_p, l, n_bp, seq, rnn_conv_w, conv_b3, wg, rnn_gate_b, lam3)
        y_rnn_s, hn_s, cn_s = _rglru_sample(pa_s, l, n_bs, n_s, conv_state, state_rnn_h, rnn_conv_w,
                                            conv_b3, wg, rnn_gate_b, lam3)
        y_pool_p, last_p = _pool_prompt(pa_p, l, n_bp, seq, pool_w, pool_scale3)
        y_pool_s, pn_s = _pool_sample(pa_s, l, n_bs, n_s, pool_state, pool_w, pool_scale3)

        prompt_specs = [pl.BlockSpec((None, KV_ROWS, page), functools.partial(
            lambda b, pt, p: (b, 0, pt[b, p]), p=p)) for p in range(n_pages)]
        c_p = _compress(page_ident, kv_p, prompt_specs, n_bp, l, w1c, pe_term, w2t)
        cache_specs = [pl.BlockSpec((None, KV_ROWS, page), functools.partial(
            lambda b, pt, p: (l * n_pool + pt[b, p], 0, 0), p=p)) for p in range(n_pages)]
        c_s = _compress(page_table, cache_cmp_t, cache_specs, n_bs, l, w1c, pe_term, w2t)

        y_attn_p = _attn_prompt_km(q_p, kv_p, c_p, pa_p, n_bp, seq)
        rows_s = N_KV * n_s * HEADS_PER_KV
        q_b = q_s.reshape(n_s, n_bs, N_KV, HEADS_PER_KV, HEAD_DIM)
        q_b = jnp.transpose(q_b, (1, 2, 0, 3, 4)).reshape(n_bs, rows_s, HEAD_DIM)
        q_b = jnp.concatenate([q_b, q_b], axis=-1)
        new_kv = jnp.transpose(kv_s, (2, 0, 1))
        new_kv = jnp.pad(new_kv, ((0, 0), (0, 16 - n_s), (0, 0)))
        g_s = pa_s[:, 6 * D_MODEL:6 * D_MODEL + 3 * N_HEADS].reshape(n_s, n_bs, N_KV, HEADS_PER_KV, 3)
        g_s = jnp.transpose(g_s, (1, 2, 0, 3, 4)).reshape(n_bs, rows_s, 3)
        slc_specs = [pl.BlockSpec((None, KV_ROWS, page), functools.partial(
            lambda b, pt, p: (l * n_pool + pt[b, p], 0, 0), p=p)) for p in range(n_pages)]
        o_s = _attn_sample(page_table, cache_slc_t, slc_specs, q_b, win_flat, new_kv, c_s, g_s, e_mat, l,
                           n_bs, n_s, past)
        y_attn_s = o_s[:, :, :HEAD_DIM].reshape(n_bs, N_KV, n_s, HEADS_PER_KV, HEAD_DIM)
        y_attn_s = jnp.transpose(y_attn_s, (2, 0, 1, 3, 4)).reshape(n_s * n_bs, ATTN_W)

        xp = _merge(xp, mods, l, y_rnn_p, y_pool_p, y_attn_p, pa_p, w_stack, ln_g4, ln_b4, tm=512,
                    alpha=alpha, **mod_p)
        xs = _merge(xs, mods, l, y_rnn_s, y_pool_s, y_attn_s, pa_s, w_stack, ln_g4, ln_b4, tm=tm_s,
                    alpha=alpha, **mod_s)

        xp = _ffn(xp, mods, l, 2, ffn_w_up, ffn_w_down, ln_g4, ln_b4, 2, tm=tm_p, alpha=alpha, **mod_p)
        xs = _ffn(xs, mods, l, 2, ffn_w_up, ffn_w_down, ln_g4, ln_b4, 2, tm=tm_s, alpha=alpha, **mod_s)

        kvp = kv_p.reshape(n_bp, 3, 2, N_KV, HEAD_DIM, seq)
        kvs = kv_s.reshape(n_s, 3, 2, N_KV, HEAD_DIM, n_bs)
        outs["cmp_p"].append(kvp[:, 0])
        outs["slc_p"].append(kvp[:, 1])
        outs["win_p"].append(kvp[:, 2, :, :, :, seq - min(WINDOW, seq):])
        outs["cmp_s"].append(kvs[:, 0])
        outs["slc_s"].append(kvs[:, 1])
        outs["win_s"].append(jnp.transpose(kvs[:, 2], (4, 1, 2, 3, 0)))
        outs["h_p"].append(hl_p.reshape(n_bp, D_RNN))
        outs["h_s"].append(hn_s)
        outs["conv_p"].append(jnp.swapaxes(ul_p[:, 8 - (CONV_W - 1):, :], 0, 1))
        outs["conv_s"].append(cn_s)
        outs["pool_p"].append(jnp.swapaxes(last_p[:, 16 - POOL_BUF:, :], 0, 1))
        outs["pool_s"].append(pn_s)

    def kv_out_p(lst):
        return jnp.transpose(jnp.stack(lst), (0, 1, 5, 2, 3, 4))

    def kv_out_s(lst):
        return jnp.transpose(jnp.stack(lst), (0, 5, 1, 2, 3, 4))

    win_old = win_t.reshape(depth, n_bs, 2, N_KV, HEAD_DIM, wlen)
    win_s_t = jnp.concatenate([win_old[..., n_s:], jnp.stack(outs["win_s"])], axis=-1)
    win_s_out = jnp.transpose(win_s_t, (0, 1, 5, 2, 3, 4))

    y_prompt = xp.reshape(n_bp, seq, D_MODEL)
    y_sample = jnp.swapaxes(xs.reshape(n_s, n_bs, D_MODEL), 0, 1)
    return (y_prompt, y_sample,
            kv_out_p(outs["cmp_p"]), kv_out_s(outs["cmp_s"]),
            kv_out_p(outs["slc_p"]), kv_out_s(outs["slc_s"]),
            kv_out_p(outs["win_p"]), win_s_out,
            jnp.stack(outs["h_p"]), jnp.stack(outs["h_s"]),
            jnp.swapaxes(jnp.stack(outs["conv_p"]), 1, 2), jnp.swapaxes(jnp.stack(outs["conv_s"]), 1, 2),
            jnp.swapaxes(jnp.stack(outs["pool_p"]), 1, 2), jnp.swapaxes(jnp.stack(outs["pool_s"]), 1, 2))
```

```python
import functools
import math

import jax
import jax.numpy as jnp
import numpy as np
from jax import lax
from jax.experimental import pallas as pl
from jax.experimental.pallas import tpu as pltpu

D_MODEL = 1024
N_HEADS = 16
N_KV = 4
HEADS_PER_KV = N_HEADS // N_KV
HEAD_DIM = 64
ROT_DIM = HEAD_DIM // 4
ROT_HALF = ROT_DIM // 2
ROPE_THETA = 500000.0
CMP_BLOCK = 32
CMP_STRIDE = 16
CMP_HIDDEN = 128
SLC_BLOCK = 64
N_SELECT = 8
WINDOW = 512
ATTN_SCALE = HEAD_DIM ** -0.5
D_RNN = D_MODEL
N_RNN_BLOCKS = 16
RNN_BW = D_RNN // N_RNN_BLOCKS
CONV_W = 4
RG_C = 8.0
D_POOL = D_MODEL
POOL_WINDOWS = (2, 4, 8, 16)
POOL_GW = D_POOL // len(POOL_WINDOWS)
POOL_BUF = max(POOL_WINDOWS) - 1
D_FF = 2816
ATTN_W = N_HEADS * HEAD_DIM
KV_W = 3 * 2 * N_KV * HEAD_DIM
LN_EPS = 1e-5

LANE = 128
KV_ROWS = 2 * N_KV * HEAD_DIM
K_ROWS = N_KV * HEAD_DIM
QKV_ROWS = ATTN_W + KV_W
PROJ_A_COLS = 3 * D_MODEL + 3 * D_MODEL + LANE
GATE_COL_BLOCK = (6 * D_MODEL) // LANE
NEG_BIG = -1e30
VMEM_LIMIT = 56 * 1024 * 1024

F32 = jnp.float32
BF16 = jnp.bfloat16


def _bf(x):
    return x.astype(BF16)


def _dot(a, b):
    return jnp.dot(a, b, preferred_element_type=F32)


def _dot_nt(a, b):
    return lax.dot_general(a, b, (((1,), (1,)), ((), ())), preferred_element_type=F32)


def _gelu(x):
    c = math.sqrt(2.0 / math.pi)
    return x * (0.5 * (1.0 + jnp.tanh(c * (x + 0.044715 * (x * x * x)))))


def _sigmoid(x):
    return 1.0 / (1.0 + jnp.exp(-x))


def _sigmoid_t(x):
    return 0.5 * jnp.tanh(0.5 * x) + 0.5


def _silu(x):
    return x * _sigmoid(x)


def _layer_norm(z, g, b):
    mu = jnp.mean(z, axis=-1, keepdims=True)
    zc = z - mu
    var = jnp.mean(zc * zc, axis=-1, keepdims=True)
    return zc * lax.rsqrt(var + LN_EPS) * g + b


def _params(sem):
    return pltpu.CompilerParams(dimension_semantics=sem, vmem_limit_bytes=VMEM_LIMIT)


def _mod_rows(m_ref, j, per_row, b_idx, reps):
    if per_row:
        m = m_ref[j]
        return jnp.concatenate([m] * reps, axis=0) if reps > 1 else m
    return m_ref[j, pl.ds(b_idx, 1), :]


def _adaln_body(c_ref, w_ref, b_ref, o_ref):
    sc = _silu(c_ref[...])
    o_ref[...] = _dot(_bf(sc), _bf(w_ref[...])) + b_ref[...]


def _adaln(c_all, ada_w, ada_b):
    depth = ada_w.shape[0]
    rows = c_all.shape[0]
    b4 = ada_b.reshape(depth, 9, 1, D_MODEL)
    return pl.pallas_call(
        _adaln_body,
        grid=(depth, 9),
        in_specs=[
            pl.BlockSpec((rows, D_MODEL), lambda l, k: (0, 0)),
            pl.BlockSpec((None, D_MODEL, D_MODEL), lambda l, k: (l, 0, k)),
            pl.BlockSpec((None, None, 1, D_MODEL), lambda l, k: (l, k, 0, 0)),
        ],
        out_specs=pl.BlockSpec((None, None, rows, D_MODEL), lambda l, k: (l, k, 0, 0)),
        out_shape=jax.ShapeDtypeStruct((depth, 9, rows, D_MODEL), F32),
        compiler_params=_params(("arbitrary", "arbitrary")),
        name="adaln",
    )(c_all, ada_w, b4)


def _ffn_body(x_ref, m_ref, wa_ref, wv_ref, wd_ref, g_ref, b_ref, o_ref, h_scr, acc_scr,
              *, per_row, tiles_per_b, reps, alpha):
    i = pl.program_id(0)
    j = pl.program_id(1)
    b_idx = i // tiles_per_b

    @pl.when(j == 0)
    def _():
        shift = _mod_rows(m_ref, 0, per_row, b_idx, reps)
        scale = _mod_rows(m_ref, 1, per_row, b_idx, reps)
        h_scr[...] = _bf(x_ref[...] * (1.0 + scale) + shift)
        acc_scr[...] = jnp.zeros_like(acc_scr)

    h = h_scr[...]
    a = _dot(h, _bf(wa_ref[...]))
    v = _dot(h, _bf(wv_ref[...]))
    act = _bf(_silu(a) * v)
    acc_scr[...] += _dot(act, _bf(wd_ref[...]))

    @pl.when(j == pl.num_programs(1) - 1)
    def _():
        gate = _mod_rows(m_ref, 2, per_row, b_idx, reps)
        z = alpha * x_ref[...] + 0.5 * gate * acc_scr[...]
        o_ref[...] = _layer_norm(z, g_ref[...], b_ref[...])


def _ffn(x, mods, l, k, w_up, w_down, ln_g, ln_b, ln_idx, *, per_row, tm, rows_per_b, mod_row_block,
         mod_rows, alpha):
    m_tok = x.shape[0]
    tf = 256
    n_f = D_FF // tf
    reps = tm // mod_rows if per_row else 1
    tiles_per_b = max(rows_per_b // tm, 1)
    body = functools.partial(_ffn_body, per_row=per_row, tiles_per_b=tiles_per_b, reps=reps, alpha=alpha)
    w_idx = 0 if k == 0 else 1
    return pl.pallas_call(
        body,
        grid=(m_tok // tm, n_f),
        in_specs=[
            pl.BlockSpec((tm, D_MODEL), lambda i, j: (i, 0)),
            pl.BlockSpec((None, 3, mod_rows, D_MODEL), lambda i, j: (l, k, mod_row_block, 0)),
            pl.BlockSpec((None, None, D_MODEL, tf), lambda i, j: (l, w_idx, 0, j)),
            pl.BlockSpec((None, None, D_MODEL, tf), lambda i, j: (l, w_idx, 0, n_f + j)),
            pl.BlockSpec((None, None, tf, D_MODEL), lambda i, j: (l, w_idx, j, 0)),
            pl.BlockSpec((None, None, 1, D_MODEL), lambda i, j: (l, ln_idx, 0, 0)),
            pl.BlockSpec((None, None, 1, D_MODEL), lambda i, j: (l, ln_idx, 0, 0)),
        ],
        out_specs=pl.BlockSpec((tm, D_MODEL), lambda i, j: (i, 0)),
        out_shape=jax.ShapeDtypeStruct((m_tok, D_MODEL), F32),
        scratch_shapes=[pltpu.VMEM((tm, D_MODEL), BF16), pltpu.VMEM((tm, D_MODEL), F32)],
        compiler_params=_params(("arbitrary", "arbitrary")),
        name="ffn",
    )(x, mods, w_up, w_up, w_down, ln_g, ln_b)


def _proj_a_body(x_ref, m_ref, w_ref, b_ref, o_ref, h_scr, *, per_row, tiles_per_b, reps):
    i = pl.program_id(0)
    j = pl.program_id(1)

    @pl.when(j == 0)
    def _():
        b_idx = i // tiles_per_b
        shift = _mod_rows(m_ref, 0, per_row, b_idx, reps)
        scale = _mod_rows(m_ref, 1, per_row, b_idx, reps)
        h_scr[...] = _bf(x_ref[...] * (1.0 + scale) + shift)

    o_ref[...] = _dot_nt(h_scr[...], _bf(w_ref[...])) + b_ref[...]


def _proj_a(x, mods, l, wt_a, bias_a, *, per_row, tm, rows_per_b, mod_row_block, mod_rows):
    m_tok = x.shape[0]
    tn = 896
    reps = tm // mod_rows if per_row else 1
    tiles_per_b = max(rows_per_b // tm, 1)
    body = functools.partial(_proj_a_body, per_row=per_row, tiles_per_b=tiles_per_b, reps=reps)
    return pl.pallas_call(
        body,
        grid=(m_tok // tm, PROJ_A_COLS // tn),
        in_specs=[
            pl.BlockSpec((tm, D_MODEL), lambda i, j: (i, 0)),
            pl.BlockSpec((None, 3, mod_rows, D_MODEL), lambda i, j: (l, 1, mod_row_block, 0)),
            pl.BlockSpec((None, tn, D_MODEL), lambda i, j: (l, j, 0)),
            pl.BlockSpec((None, 1, tn), lambda i, j: (l, 0, j)),
        ],
        out_specs=pl.BlockSpec((tm, tn), lambda i, j: (i, j)),
        out_shape=jax.ShapeDtypeStruct((m_tok, PROJ_A_COLS), F32),
        scratch_shapes=[pltpu.VMEM((tm, D_MODEL), BF16)],
        compiler_params=_params(("arbitrary", "arbitrary")),
        name="proj_a",
    )(x, mods, wt_a, bias_a)


def _rope_rows(res, cos, sin, row0):
    x1 = res[row0:row0 + ROT_HALF, :]
    x2 = res[row0 + ROT_HALF:row0 + ROT_DIM, :]
    return x1 * cos - x2 * sin, x2 * cos + x1 * sin


def _proj_b_body(x_ref, m_ref, w_ref, b_ref, cos_ref, sin_ref, o_ref, h_scr,
                 *, per_row, tiles_per_b, reps, n_sub, tf):
    i = pl.program_id(0)
    j = pl.program_id(1)

    @pl.when(j == 0)
    def _():
        b_idx = i // tiles_per_b
        shift = _mod_rows(m_ref, 0, per_row, b_idx, reps)
        scale = _mod_rows(m_ref, 1, per_row, b_idx, reps)
        h_scr[...] = _bf(x_ref[...] * (1.0 + scale) + shift)

    res = _dot_nt(_bf(w_ref[...]), h_scr[...]) + b_ref[...]
    sub = res.shape[1] // n_sub

    def store(rows, val):
        if n_sub == 1:
            o_ref[rows, :] = val
        else:
            for s in range(n_sub):
                o_ref[s, rows, :] = val[:, s * sub:(s + 1) * sub]

    store(slice(0, tf), res)

    @pl.when(j > 0)
    def _():
        cos = cos_ref[...]
        sin = sin_ref[...]
        for hh in range(N_KV):
            r0 = hh * HEAD_DIM
            n1, n2 = _rope_rows(res, cos, sin, r0)
            store(slice(r0, r0 + ROT_HALF), n1)
            store(slice(r0 + ROT_HALF, r0 + ROT_DIM), n2)


def _proj_b(x, mods, l, wt_full, bias_b, cos_t, sin_t, *, per_row, tm, rows_per_b, mod_row_block,
            mod_rows, n_sub):
    m_tok = x.shape[0]
    tf = KV_ROWS
    n_f = KV_W // tf
    row_block0 = (3 * D_MODEL + ATTN_W) // tf
    reps = tm // mod_rows if per_row else 1
    tiles_per_b = max(rows_per_b // tm, 1)
    body = functools.partial(_proj_b_body, per_row=per_row, tiles_per_b=tiles_per_b, reps=reps,
                             n_sub=n_sub, tf=tf)
    if n_sub == 1:
        n_b = m_tok // rows_per_b
        out_shape = jax.ShapeDtypeStruct((n_b, KV_W, rows_per_b), F32)
        out_spec = pl.BlockSpec((None, tf, tm), lambda i, j: (i // tiles_per_b, j, i % tiles_per_b))
        cs_spec = pl.BlockSpec((ROT_HALF, tm), lambda i, j: (0, i % tiles_per_b))
    else:
        sub = m_tok // n_sub
        out_shape = jax.ShapeDtypeStruct((n_sub, KV_W, sub), F32)
        out_spec = pl.BlockSpec((n_sub, tf, sub), lambda i, j: (0, j, 0))
        cs_spec = pl.BlockSpec((ROT_HALF, tm), lambda i, j: (0, 0))
    return pl.pallas_call(
        body,
        grid=(m_tok // tm, n_f),
        in_specs=[
            pl.BlockSpec((tm, D_MODEL), lambda i, j: (i, 0)),
            pl.BlockSpec((None, 3, mod_rows, D_MODEL), lambda i, j: (l, 1, mod_row_block, 0)),
            pl.BlockSpec((None, tf, D_MODEL), lambda i, j: (l, row_block0 + j, 0)),
            pl.BlockSpec((None, tf, 1), lambda i, j: (l, j, 0)),
            cs_spec,
            cs_spec,
        ],
        out_specs=out_spec,
        out_shape=out_shape,
        scratch_shapes=[pltpu.VMEM((tm, D_MODEL), BF16)],
        compiler_params=_params(("arbitrary", "arbitrary")),
        name="proj_b",
    )(x, mods, wt_full, bias_b, cos_t, sin_t)


def _proj_q_body(x_ref, m_ref, w_ref, b_ref, c_ref, slo_ref, shi_ref, o_ref, *, per_row, tiles_per_b, reps):
    i = pl.program_id(0)
    b_idx = i // tiles_per_b
    shift = _mod_rows(m_ref, 0, per_row, b_idx, reps)
    scale = _mod_rows(m_ref, 1, per_row, b_idx, reps)
    h = _bf(x_ref[...] * (1.0 + scale) + shift)
    q = _dot_nt(h, _bf(w_ref[...])) + b_ref[...]
    width = q.shape[1]
    n_rep = width // LANE
    cos = jnp.concatenate([c_ref[...]] * n_rep, axis=1)
    s_lo = jnp.concatenate([slo_ref[...]] * n_rep, axis=1)
    s_hi = jnp.concatenate([shi_ref[...]] * n_rep, axis=1)
    o_ref[...] = (q * cos + pltpu.roll(q, ROT_HALF, axis=1) * s_lo
                  + pltpu.roll(q, width - ROT_HALF, axis=1) * s_hi)


def _proj_q(x, mods, l, wt_full, bias_q, rope_tabs, *, per_row, tm, rows_per_b, mod_row_block, mod_rows):
    m_tok = x.shape[0]
    reps = tm // mod_rows if per_row else 1
    tiles_per_b = max(rows_per_b // tm, 1)
    row_block0 = (3 * D_MODEL) // ATTN_W
    body = functools.partial(_proj_q_body, per_row=per_row, tiles_per_b=tiles_per_b, reps=reps)
    if per_row:
        tab_spec = pl.BlockSpec((tm, LANE), lambda i: (0, 0))
    else:
        tab_spec = pl.BlockSpec((tm, LANE), lambda i: (i % tiles_per_b, 0))
    return pl.pallas_call(
        body,
        grid=(m_tok // tm,),
        in_specs=[
            pl.BlockSpec((tm, D_MODEL), lambda i: (i, 0)),
            pl.BlockSpec((None, 3, mod_rows, D_MODEL), lambda i: (l, 1, mod_row_block, 0)),
            pl.BlockSpec((None, ATTN_W, D_MODEL), lambda i: (l, row_block0, 0)),
            pl.BlockSpec((None, 1, ATTN_W), lambda i: (l, 0, 0)),
            tab_spec, tab_spec, tab_spec,
        ],
        out_specs=pl.BlockSpec((tm, ATTN_W), lambda i: (i, 0)),
        out_shape=jax.ShapeDtypeStruct((m_tok, ATTN_W), F32),
        compiler_params=_params(("arbitrary",)),
        name="proj_q",
    )(x, mods, wt_full, bias_q, *rope_tabs)


def _rglru_gates(xc, wg_ref, gb_ref, lam_ref):
    nblk = D_RNN // 256
    r_parts, i_parts = [], []
    for blk in range(nblk):
        xb = _bf(xc[:, blk * 256:(blk + 1) * 256])
        r_parts.append(_dot(xb, _bf(wg_ref[0, blk])))
        i_parts.append(_dot(xb, _bf(wg_ref[1, blk])))
    r = _sigmoid_t(jnp.concatenate(r_parts, axis=1) + gb_ref[0:1, :])
    ig = _sigmoid_t(jnp.concatenate(i_parts, axis=1) + gb_ref[1:2, :])
    lam = lam_ref[...]
    softplus_neg = jnp.maximum(-lam, 0.0) + jnp.log(1.0 + jnp.exp(-jnp.abs(lam)))
    log_a = -RG_C * r * softplus_neg
    a = jnp.exp(log_a)
    b = jnp.sqrt(1.0 - a * a) * (ig * xc)
    return a, b


def _rglru_prompt_body(u_ref, g_ref, cw_ref, cb_ref, wg_ref, gb_ref, lam_ref,
                       y_ref, hl_ref, ul_ref, h_scr, c_scr, *, tt):
    t = pl.program_id(1)

    @pl.when(t == 0)
    def _():
        h_scr[...] = jnp.zeros_like(h_scr)
        c_scr[...] = jnp.zeros_like(c_scr)

    u = u_ref[...]
    ext = jnp.concatenate([c_scr[...], u], axis=0)
    xc = cb_ref[...] + u * cw_ref[CONV_W - 1:CONV_W, :]
    for back in range(1, CONV_W):
        sh = pltpu.roll(ext, back, axis=0)[8:8 + tt]
        xc = xc + sh * cw_ref[CONV_W - 1 - back:CONV_W - back, :]
    a, b = _rglru_gates(xc, wg_ref, gb_ref, lam_ref)

    sub_row = lax.broadcasted_iota(jnp.int32, (tt, 1), 0) % 8
    d = 1
    while d < 8:
        a_sh = pltpu.roll(a, d, axis=0)
        b_sh = pltpu.roll(b, d, axis=0)
        valid = sub_row >= d
        b = jnp.where(valid, a * b_sh + b, b)
        a = jnp.where(valid, a * a_sh, a)
        d *= 2
    h_prev = h_scr[0:1, :]
    h_groups = []
    for k in range(tt // 8):
        h_k = b[8 * k:8 * k + 8, :] + a[8 * k:8 * k + 8, :] * h_prev
        h_groups.append(h_k)
        h_prev = h_k[7:8, :]
    h = jnp.concatenate(h_groups, axis=0)
    y_ref[...] = h * _gelu(g_ref[...])
    h_scr[...] = jnp.broadcast_to(h[tt - 1:tt, :], h_scr.shape)
    c_scr[...] = u[tt - 8:tt, :]

    @pl.when(t == pl.num_programs(1) - 1)
    def _():
        hl_ref[...] = h[tt - 1:tt, :]
        ul_ref[...] = u[tt - 8:tt, :]


def _rglru_prompt(proj_a, l, n_b, seq, conv_w, conv_b, wg, gate_b, lam):
    tt = 256
    nt = seq // tt
    body = functools.partial(_rglru_prompt_body, tt=tt)
    return pl.pallas_call(
        body,
        grid=(n_b, nt),
        in_specs=[
            pl.BlockSpec((tt, D_RNN), lambda b, t: (b * nt + t, 0)),
            pl.BlockSpec((tt, D_RNN), lambda b, t: (b * nt + t, 1)),
            pl.BlockSpec((None, CONV_W, D_RNN), lambda b, t: (l, 0, 0)),
            pl.BlockSpec((None, 1, D_RNN), lambda b, t: (l, 0, 0)),
            pl.BlockSpec((None, 2, D_RNN // 256, 256, 256), lambda b, t: (l, 0, 0, 0, 0)),
            pl.BlockSpec((None, 2, D_RNN), lambda b, t: (l, 0, 0)),
            pl.BlockSpec((None, 1, D_RNN), lambda b, t: (l, 0, 0)),
        ],
        out_specs=[
            pl.BlockSpec((tt, D_RNN), lambda b, t: (b * nt + t, 0)),
            pl.BlockSpec((None, 1, D_RNN), lambda b, t: (b, 0, 0)),
            pl.BlockSpec((None, 8, D_RNN), lambda b, t: (b, 0, 0)),
        ],
        out_shape=[
            jax.ShapeDtypeStruct((n_b * seq, D_RNN), F32),
            jax.ShapeDtypeStruct((n_b, 1, D_RNN), F32),
            jax.ShapeDtypeStruct((n_b, 8, D_RNN), F32),
        ],
        scratch_shapes=[pltpu.VMEM((8, D_RNN), F32), pltpu.VMEM((8, D_RNN), F32)],
        compiler_params=_params(("arbitrary", "arbitrary")),
        name="rglru_prompt",
    )(proj_a, proj_a, conv_w, conv_b, wg, gate_b, lam)


def _rglru_sample_body(u_ref, g_ref, buf_ref, h0_ref, cw_ref, cb_ref, wg_ref, gb_ref, lam_ref,
                       y_ref, hn_ref, cn_ref, *, n_b, n_s):
    u = u_ref[...]
    slabs = [buf_ref[r] for r in range(CONV_W - 1)] + [u[s * n_b:(s + 1) * n_b] for s in range(n_s)]
    xcs = []
    for s in range(n_s):
        xc = cb_ref[...] + slabs[s] * cw_ref[0:1, :]
        for k in range(1, CONV_W):
            xc = xc + slabs[s + k] * cw_ref[k:k + 1, :]
        xcs.append(xc)
    xc = jnp.concatenate(xcs, axis=0)
    a, b = _rglru_gates(xc, wg_ref, gb_ref, lam_ref)
    h = h0_ref[...]
    hs = []
    for s in range(n_s):
        h = a[s * n_b:(s + 1) * n_b] * h + b[s * n_b:(s + 1) * n_b]
        hs.append(h)
    y_ref[...] = jnp.concatenate(hs, axis=0) * _gelu(g_ref[...])
    hn_ref[...] = h
    for r in range(CONV_W - 1):
        cn_ref[r] = slabs[n_s + r]


def _rglru_sample(proj_a, l, n_b, n_s, conv_buf, h0, conv_w, conv_b, wg, gate_b, lam):
    m_tok = n_b * n_s
    body = functools.partial(_rglru_sample_body, n_b=n_b, n_s=n_s)
    return pl.pallas_call(
        body,
        grid=(1,),
        in_specs=[
            pl.BlockSpec((m_tok, D_RNN), lambda i: (0, 0)),
            pl.BlockSpec((m_tok, D_RNN), lambda i: (0, 1)),
            pl.BlockSpec((None, CONV_W - 1, n_b, D_RNN), lambda i: (l, 0, 0, 0)),
            pl.BlockSpec((None, n_b, D_RNN), lambda i: (l, 0, 0)),
            pl.BlockSpec((None, CONV_W, D_RNN), lambda i: (l, 0, 0)),
            pl.BlockSpec((None, 1, D_RNN), lambda i: (l, 0, 0)),
            pl.BlockSpec((None, 2, D_RNN // 256, 256, 256), lambda i: (l, 0, 0, 0, 0)),
            pl.BlockSpec((None, 2, D_RNN), lambda i: (l, 0, 0)),
            pl.BlockSpec((None, 1, D_RNN), lambda i: (l, 0, 0)),
        ],
        out_specs=[
            pl.BlockSpec((m_tok, D_RNN), lambda i: (0, 0)),
            pl.BlockSpec((n_b, D_RNN), lambda i: (0, 0)),
            pl.BlockSpec((CONV_W - 1, n_b, D_RNN), lambda i: (0, 0, 0)),
        ],
        out_shape=[
            jax.ShapeDtypeStruct((m_tok, D_RNN), F32),
            jax.ShapeDtypeStruct((n_b, D_RNN), F32),
            jax.ShapeDtypeStruct((CONV_W - 1, n_b, D_RNN), F32),
        ],
        compiler_params=_params(("arbitrary",)),
        name="rglru_sample",
    )(proj_a, proj_a, conv_buf, h0, conv_w, conv_b, wg, gate_b, lam)


def _pool_prompt_body(u_ref, w_ref, sc_ref, y_ref, last_ref, halo_scr, *, tt):
    t = pl.program_id(1)

    @pl.when(t == 0)
    def _():
        halo_scr[...] = jnp.zeros_like(halo_scr)

    x = u_ref[...]
    ext = jnp.concatenate([halo_scr[...], x], axis=0)
    pos = t * tt + lax.broadcasted_iota(jnp.int32, (tt, 1), 0)
    outs = []
    for gi, w in enumerate(POOL_WINDOWS):
        cols = slice(gi * POOL_GW, (gi + 1) * POOL_GW)
        s = ext[:, cols]
        k = 1
        while k < w:
            s = s + pltpu.roll(s, k, axis=0)
            k *= 2
        cnt = jnp.minimum(pos + 1, w).astype(F32)
        d = s[16:16 + tt] / cnt - x[:, cols]
        outs.append(_dot(_bf(d), _bf(w_ref[gi])))
    y_ref[...] = jnp.concatenate(outs, axis=1) * sc_ref[...]
    halo_scr[...] = x[tt - 16:tt, :]

    @pl.when(t == pl.num_programs(1) - 1)
    def _():
        last_ref[...] = x[tt - 16:tt, :]


def _pool_prompt(proj_a, l, n_b, seq, pool_w, pool_scale):
    tt = 256
    nt = seq // tt
    body = functools.partial(_pool_prompt_body, tt=tt)
    return pl.pallas_call(
        body,
        grid=(n_b, nt),
        in_specs=[
            pl.BlockSpec((tt, D_POOL), lambda b, t: (b * nt + t, 2)),
            pl.BlockSpec((None, len(POOL_WINDOWS), POOL_GW, POOL_GW), lambda b, t: (l, 0, 0, 0)),
            pl.BlockSpec((None, 1, D_POOL), lambda b, t: (l, 0, 0)),
        ],
        out_specs=[
            pl.BlockSpec((tt, D_POOL), lambda b, t: (b * nt + t, 0)),
            pl.BlockSpec((None, 16, D_POOL), lambda b, t: (b, 0, 0)),
        ],
        out_shape=[
            jax.ShapeDtypeStruct((n_b * seq, D_POOL), F32),
            jax.ShapeDtypeStruct((n_b, 16, D_POOL), F32),
        ],
        scratch_shapes=[pltpu.VMEM((16, D_POOL), F32)],
        compiler_params=_params(("arbitrary", "arbitrary")),
        name="pool_prompt",
    )(proj_a, pool_w, pool_scale)


def _pool_sample_body(u_ref, buf_ref, w_ref, sc_ref, y_ref, pn_ref, *, n_b, n_s):
    u = u_ref[...]
    slabs = [buf_ref[r] for r in range(POOL_BUF)] + [u[s * n_b:(s + 1) * n_b] for s in range(n_s)]
    outs = []
    for gi, w in enumerate(POOL_WINDOWS):
        cols = slice(gi * POOL_GW, (gi + 1) * POOL_GW)
        ds = []
        for s in range(n_s):
            acc = slabs[POOL_BUF + s][:, cols]
            for back in range(1, w):
                acc = acc + slabs[POOL_BUF + s - back][:, cols]
            ds.append(acc / float(w) - slabs[POOL_BUF + s][:, cols])
        outs.append(_dot(_bf(jnp.concatenate(ds, axis=0)), _bf(w_ref[gi])))
    y_ref[...] = jnp.concatenate(outs, axis=1) * sc_ref[...]
    for r in range(POOL_BUF):
        pn_ref[r] = slabs[n_s + r]


def _pool_sample(proj_a, l, n_b, n_s, pool_buf, pool_w, pool_scale):
    m_tok = n_b * n_s
    body = functools.partial(_pool_sample_body, n_b=n_b, n_s=n_s)
    return pl.pallas_call(
        body,
        grid=(1,),
        in_specs=[
            pl.BlockSpec((m_tok, D_POOL), lambda i: (0, 2)),
            pl.BlockSpec((None, POOL_BUF, n_b, D_POOL), lambda i: (l, 0, 0, 0)),
            pl.BlockSpec((None, len(POOL_WINDOWS), POOL_GW, POOL_GW), lambda i: (l, 0, 0, 0)),
            pl.BlockSpec((None, 1, D_POOL), lambda i: (l, 0, 0)),
        ],
        out_specs=[
            pl.BlockSpec((m_tok, D_POOL), lambda i: (0, 0)),
            pl.BlockSpec((POOL_BUF, n_b, D_POOL), lambda i: (0, 0, 0)),
        ],
        out_shape=[
            jax.ShapeDtypeStruct((m_tok, D_POOL), F32),
            jax.ShapeDtypeStruct((POOL_BUF, n_b, D_POOL), F32),
        ],
        compiler_params=_params(("arbitrary",)),
        name="pool_sample",
    )(proj_a, pool_buf, pool_w, pool_scale)


def _pe_term_body(pe_ref, w1_ref, b1_ref, o_ref):
    pe = jnp.broadcast_to(pe_ref[...], (8, pe_ref.shape[1]))
    o_ref[...] = _dot(_bf(pe), _bf(w1_ref[...]))[0:1, :] + b1_ref[...]


def _pe_term(cmp_pe, cmp_w1, cmp_b1):
    depth = cmp_pe.shape[0]
    flat = CMP_BLOCK * HEAD_DIM
    pe4 = cmp_pe.reshape(depth, 2, 1, flat)
    b4 = cmp_b1.reshape(depth, 2, 1, CMP_HIDDEN)
    return pl.pallas_call(
        _pe_term_body,
        grid=(depth, 2),
        in_specs=[
            pl.BlockSpec((None, None, 1, flat), lambda l, c: (l, c, 0, 0)),
            pl.BlockSpec((None, None, flat, CMP_HIDDEN), lambda l, c: (l, c, 0, 0)),
            pl.BlockSpec((None, None, 1, CMP_HIDDEN), lambda l, c: (l, c, 0, 0)),
        ],
        out_specs=pl.BlockSpec((None, None, 1, CMP_HIDDEN), lambda l, c: (l, c, 0, 0)),
        out_shape=jax.ShapeDtypeStruct((depth, 2, 1, CMP_HIDDEN), F32),
        compiler_params=_params(("arbitrary", "arbitrary")),
        name="pe_term",
    )(pe4, cmp_w1, b4)


COMPRESS_SEQS_PER_STEP = 2


def _compress_body(*refs, n_pages, page, n_sq):
    page_refs = refs[1:1 + n_sq * n_pages]
    w1_ref, pet_ref, w2t_ref, o_ref, s_scr, x_scr = refs[1 + n_sq * n_pages:]
    n_chunks = n_pages * page // CMP_STRIDE
    n_slab = KV_ROWS // LANE
    for sq in range(n_sq):
        for p in range(n_pages):
            for j in range(n_slab):
                blk = page_refs[sq * n_pages + p][j * LANE:(j + 1) * LANE, :]
                s_scr[sq, j, p * page:(p + 1) * page, :] = blk.T
    lane = lax.broadcasted_iota(jnp.int32, (n_chunks, LANE), 1)
    low = lane < HEAD_DIM
    for sq in range(n_sq):
        for c in range(2):
            for gp in range(2):
                j = 2 * c + gp
                for r2 in range(CMP_STRIDE // 2):
                    v0 = s_scr[sq, j, pl.ds(2 * r2, n_chunks, stride=CMP_STRIDE), :]
                    v1 = s_scr[sq, j, pl.ds(2 * r2 + 1, n_chunks, stride=CMP_STRIDE), :]
                    xe = jnp.where(low, v0, pltpu.roll(v1, HEAD_DIM, axis=1))
                    xo = jnp.where(low, pltpu.roll(v0, HEAD_DIM, axis=1), v1)
                    cols = slice(r2 * LANE, (r2 + 1) * LANE)
                    x_scr[sq, c, (2 * gp) * n_chunks:(2 * gp + 1) * n_chunks, cols] = _bf(xe)
                    x_scr[sq, c, (2 * gp + 1) * n_chunks:(2 * gp + 2) * n_chunks, cols] = _bf(xo)
            pm = _dot(x_scr[sq, c], _bf(w1_ref[c]))
            rows = pm.shape[0]
            pre = pm[:, :CMP_HIDDEN] + pltpu.roll(pm[:, CMP_HIDDEN:], rows - 1, axis=0)
            hid = _gelu(pre + pet_ref[c])
            out_t = _dot_nt(_bf(w2t_ref[c]), _bf(hid))
            for g in range(N_KV):
                o_ref[sq, c, g * HEAD_DIM:(g + 1) * HEAD_DIM, :] = out_t[:, g * n_chunks:(g + 1) * n_chunks]


def _compress(page_table, src, page_index, n_pages, n_seq, l, w1c, pe_term, w2t):
    n_sq = COMPRESS_SEQS_PER_STEP
    assert n_seq % n_sq == 0
    page = LANE
    n_chunks = n_pages * page // CMP_STRIDE
    body = functools.partial(_compress_body, n_pages=n_pages, page=page, n_sq=n_sq)
    page_specs = [
        pl.BlockSpec((None, KV_ROWS, page),
                     functools.partial(lambda s, pt, sq, p: page_index(n_sq * s + sq, pt, p), sq=sq, p=p))
        for sq in range(n_sq) for p in range(n_pages)]
    grid_spec = pltpu.PrefetchScalarGridSpec(
        num_scalar_prefetch=1,
        grid=(n_seq // n_sq,),
        in_specs=page_specs + [
            pl.BlockSpec((None, 2, CMP_STRIDE * HEAD_DIM, 2 * CMP_HIDDEN), lambda s, pt: (l, 0, 0, 0)),
            pl.BlockSpec((None, 2, 1, CMP_HIDDEN), lambda s, pt: (l, 0, 0, 0)),
            pl.BlockSpec((None, 2, HEAD_DIM, CMP_HIDDEN), lambda s, pt: (l, 0, 0, 0)),
        ],
        out_specs=pl.BlockSpec((n_sq, 2, K_ROWS, n_chunks), lambda s, pt: (s, 0, 0, 0)),
        scratch_shapes=[
            pltpu.VMEM((n_sq, KV_ROWS // LANE, n_pages * page, LANE), F32),
            pltpu.VMEM((n_sq, 2, N_KV * n_chunks, CMP_STRIDE * HEAD_DIM), BF16),
        ],
    )
    return pl.pallas_call(
        body,
        grid_spec=grid_spec,
        out_shape=jax.ShapeDtypeStruct((n_seq, 2, K_ROWS, n_chunks), F32),
        compiler_params=_params(("arbitrary",)),
        name="compress",
    )(page_table, *([src] * (n_sq * n_pages)), w1c, pe_term, w2t)


def _select_blocks(score_t, cur, n_blocks):
    rows = score_t.shape[0]
    j = lax.broadcasted_iota(jnp.int32, score_t.shape, 0)
    forced = (j == 0) | (j == cur) | (j == cur - 1)
    sc = jnp.where(j > cur, -jnp.inf, jnp.where(forced, jnp.inf, score_t))
    rank = jnp.zeros(score_t.shape, jnp.int32)
    for jp in range(n_blocks):
        row = sc[jp:jp + 1, :]
        ahead = (row > sc) | ((row == sc) & (jp < j))
        rank = rank + ahead.astype(jnp.int32)
    del rows
    return ((rank < N_SELECT) & (j <= cur)).astype(F32)


def _square_t(x):
    ra, cb = x.shape[0] // LANE, x.shape[1] // LANE
    rows = []
    for c in range(cb):
        rows.append(jnp.concatenate([x[r * LANE:(r + 1) * LANE, c * LANE:(c + 1) * LANE].T for r in range(ra)],
                                    axis=1) if ra > 1 else x[0:LANE, c * LANE:(c + 1) * LANE].T)
    return jnp.concatenate(rows, axis=0) if cb > 1 else rows[0]


def _rows_to_key_major(k_t):
    padded = jnp.concatenate([k_t, jnp.zeros((LANE - HEAD_DIM, k_t.shape[1]), F32)], axis=0)
    return _bf(padded.T)


def _attn_prompt_km_body(q_ref, sk_ref, sv_ref, wk_ref, wv_ref, kc_ref, vc_ref, gate_ref,
                         o_ref, ks_scr, kw_scr, imp_scr, sel_scr, g_scr, os_scr, *, tq, ck, seq):
    g = pl.program_id(1)
    i = pl.program_id(2)
    q0 = i * tq
    hp = HEADS_PER_KV
    cols = hp * tq

    @pl.when(i == 0)
    def _():
        for t in range(seq // LANE):
            ks_scr[t * LANE:(t + 1) * LANE, :] = _rows_to_key_major(sk_ref[:, t * LANE:(t + 1) * LANE])
            kw_scr[t * LANE:(t + 1) * LANE, :] = _rows_to_key_major(wk_ref[:, t * LANE:(t + 1) * LANE])

    q_t = _square_t(q_ref[...] * ATTN_SCALE)
    q_cols = jnp.concatenate([q_t[h * HEAD_DIM:(h + 1) * HEAD_DIM, :] for h in range(hp)], axis=1)
    q_cols = _bf(jnp.concatenate([q_cols, jnp.zeros((LANE - HEAD_DIM, cols), F32)], axis=0))
    qpos = q0 + lax.broadcasted_iota(jnp.int32, (1, cols), 1) % tq
    qpos_q = q0 + lax.broadcasted_iota(jnp.int32, (1, tq), 1)

    n_blk = kc_ref.shape[1]
    s_c = _dot(_rows_to_key_major(kc_ref[...]), q_cols)
    blk_end = lax.broadcasted_iota(jnp.int32, (n_blk, 1), 0) * CMP_STRIDE + (CMP_BLOCK - 1)
    cmask = blk_end <= qpos
    s_c = jnp.where(cmask, s_c, NEG_BIG)
    m_c = jnp.max(s_c, axis=0, keepdims=True)
    e_c = jnp.where(cmask, jnp.exp(s_c - m_c), 0.0)
    p_c = e_c / jnp.maximum(jnp.sum(e_c, axis=0, keepdims=True), 1e-30)
    o_c = _dot(_bf(vc_ref[...]), _bf(p_c))

    imp_t = p_c[:, 0:tq]
    for h in range(1, hp):
        imp_t = imp_t + p_c[:, h * tq:(h + 1) * tq]
    imp_scr[0:8, :] = jnp.zeros((8, tq), F32)
    imp_scr[8:8 + n_blk, :] = imp_t
    n_sel_blocks = seq // SLC_BLOCK
    ratio = SLC_BLOCK // CMP_STRIDE
    score_t = imp_scr[pl.ds(7, n_sel_blocks, stride=ratio), :]
    for o in range(1, ratio + 1):
        score_t = score_t + imp_scr[pl.ds(7 + o, n_sel_blocks, stride=ratio), :]
    sel_scr[...] = _select_blocks(score_t, qpos_q // SLC_BLOCK, n_sel_blocks)

    sub = lax.broadcasted_iota(jnp.int32, (SLC_BLOCK, 1), 0)

    def sel_branch(n_keys):
        parts = []
        for j in range(n_keys // SLC_BLOCK):
            ok = (sel_scr[j:j + 1, :] > 0.5) & (j * SLC_BLOCK + sub <= qpos_q)
            parts.append(jnp.where(ok, 0.0, NEG_BIG))
        bias = jnp.concatenate(parts, axis=0)
        s = _dot(ks_scr[0:n_keys, :], q_cols) + jnp.concatenate([bias] * hp, axis=1)
        p = jnp.exp(s - jnp.max(s, axis=0, keepdims=True))
        acc = _dot(_bf(sv_ref[:, 0:n_keys]), _bf(p))
        os_scr[...] = acc / jnp.sum(p, axis=0, keepdims=True)

    n_need = (q0 + tq + ck - 1) // ck
    for n in range(1, seq // ck + 1):
        pl.when(n_need == n)(functools.partial(sel_branch, n * ck))
    o_s = os_scr[...]

    span = WINDOW + tq
    start = pl.multiple_of(jnp.maximum(q0 - WINDOW, 0), LANE)
    dpos = qpos - (start + lax.broadcasted_iota(jnp.int32, (span, 1), 0))
    w_ok = (dpos >= 0) & (dpos < WINDOW)
    s_w = jnp.where(w_ok, _dot(kw_scr[pl.ds(start, span), :], q_cols), NEG_BIG)
    p_w = jnp.exp(s_w - jnp.max(s_w, axis=0, keepdims=True))
    o_w = _dot(_bf(wv_ref[:, pl.ds(start, span)]), _bf(p_w)) / jnp.sum(p_w, axis=0, keepdims=True)

    g_scr[...] = _sigmoid(gate_ref[...]).T
    heads = []
    for h in range(hp):
        cs = slice(h * tq, (h + 1) * tq)
        base = (g * hp + h) * 3
        acc = g_scr[pl.ds(base, 1), :] * o_c[:, cs]
        acc = acc + g_scr[pl.ds(base + 1, 1), :] * o_s[:, cs]
        acc = acc + g_scr[pl.ds(base + 2, 1), :] * o_w[:, cs]
        heads.append(acc)
    o_ref[...] = _square_t(jnp.concatenate(heads, axis=0))


def _attn_prompt_km(q_tok, kv_t, c_t, proj_a, n_b, seq):
    tq = LANE
    ck = 512
    assert seq % ck == 0 and seq >= WINDOW + tq
    nq = seq // tq
    q_w = HEADS_PER_KV * HEAD_DIM
    slc0 = KV_ROWS // HEAD_DIM
    win0 = 2 * KV_ROWS // HEAD_DIM
    body = functools.partial(_attn_prompt_km_body, tq=tq, ck=ck, seq=seq)
    n_blk = c_t.shape[-1]
    return pl.pallas_call(
        body,
        grid=(n_b, N_KV, nq),
        in_specs=[
            pl.BlockSpec((tq, q_w), lambda b, g, i: (b * nq + i, g)),
            pl.BlockSpec((None, HEAD_DIM, seq), lambda b, g, i: (b, slc0 + g, 0)),
            pl.BlockSpec((None, HEAD_DIM, seq), lambda b, g, i: (b, slc0 + N_KV + g, 0)),
            pl.BlockSpec((None, HEAD_DIM, seq), lambda b, g, i: (b, win0 + g, 0)),
            pl.BlockSpec((None, HEAD_DIM, seq), lambda b, g, i: (b, win0 + N_KV + g, 0)),
            pl.BlockSpec((None, None, HEAD_DIM, n_blk), lambda b, g, i: (b, 0, g, 0)),
            pl.BlockSpec((None, None, HEAD_DIM, n_blk), lambda b, g, i: (b, 1, g, 0)),
            pl.BlockSpec((tq, LANE), lambda b, g, i: (b * nq + i, GATE_COL_BLOCK)),
        ],
        out_specs=pl.BlockSpec((tq, q_w), lambda b, g, i: (b * nq + i, g)),
        out_shape=jax.ShapeDtypeStruct((n_b * seq, ATTN_W), F32),
        scratch_shapes=[
            pltpu.VMEM((seq, LANE), BF16),
            pltpu.VMEM((seq, LANE), BF16),
            pltpu.VMEM((8 + n_blk + 8, tq), F32),
            pltpu.VMEM((seq // SLC_BLOCK, tq), F32),
            pltpu.VMEM((LANE, tq), F32),
            pltpu.VMEM((HEAD_DIM, HEADS_PER_KV * tq), F32),
        ],
        compiler_params=_params(("arbitrary", "arbitrary", "arbitrary")),
        name="attn_prompt",
    )(q_tok, kv_t, kv_t, kv_t, kv_t, c_t, c_t, proj_a)


def _attn_sample_body(*refs, n_pages, page, n_s, past):
    pt_ref = refs[0]
    page_refs = refs[1:1 + n_pages]
    q_ref, win_ref, new_ref, c_ref, gate_ref, e_ref, o_ref, s_scr, imp_scr = refs[1 + n_pages:]
    del pt_ref
    hp = HEADS_PER_KV
    rpg = n_s * hp
    rows = N_KV * rpg
    n_keys = n_pages * page
    row_i = lax.broadcasted_iota(jnp.int32, (rows, 1), 0)
    row_g = row_i // rpg
    row_s = (row_i % rpg) // hp
    col_g = lax.broadcasted_iota(jnp.int32, (1, K_ROWS), 1) // HEAD_DIM
    q2 = q_ref[...] * ATTN_SCALE
    diag = row_g == col_g
    q_bd = _bf(jnp.where(diag, jnp.concatenate([q2] * (K_ROWS // LANE), axis=1), 0.0))

    def fold(o_full):
        z = jnp.where(diag, o_full, 0.0)
        w = z[:, 0:LANE]
        for blk in range(1, K_ROWS // LANE):
            w = w + z[:, blk * LANE:(blk + 1) * LANE]
        return w + pltpu.roll(w, HEAD_DIM, axis=1)

    n_blk = c_ref.shape[-1]
    blk_end = lax.broadcasted_iota(jnp.int32, (1, n_blk), 1) * CMP_STRIDE + (CMP_BLOCK - 1)
    cmask = blk_end <= past + row_s
    s_c = jnp.where(cmask, _dot(q_bd, _bf(c_ref[0])), NEG_BIG)
    m_c = jnp.max(s_c, axis=-1, keepdims=True)
    e_c = jnp.where(cmask, jnp.exp(s_c - m_c), 0.0)
    p_c = e_c / jnp.maximum(jnp.sum(e_c, axis=-1, keepdims=True), 1e-30)
    o_c = fold(_dot_nt(_bf(p_c), _bf(c_ref[1])))

    t1 = p_c + pltpu.roll(p_c, rows - 1, axis=0)
    imp = t1 + pltpu.roll(t1, rows - 2, axis=0)
    imp_pad = jnp.concatenate([imp, jnp.zeros((LANE - rows, n_blk), F32)], axis=0)
    imp_scr[...] = jnp.zeros(imp_scr.shape, F32)
    imp_scr[8:8 + n_blk, :] = imp_pad.T
    n_sel_rows = 40
    n_sel_blocks = (past + n_s + SLC_BLOCK - 1) // SLC_BLOCK
    ratio = SLC_BLOCK // CMP_STRIDE
    score_t = imp_scr[pl.ds(7, n_sel_rows, stride=ratio), :]
    for o in range(1, ratio + 1):
        score_t = score_t + imp_scr[pl.ds(7 + o, n_sel_rows, stride=ratio), :]
    lane_row = lax.broadcasted_iota(jnp.int32, (1, LANE), 1)
    cur = (past + (lane_row % rpg) // hp) // SLC_BLOCK
    sel_t = _select_blocks(score_t, cur, n_sel_blocks)
    sel_pad = jnp.concatenate([sel_t, jnp.zeros((LANE - n_sel_rows, LANE), F32)], axis=0)
    sel_rows = _bf(sel_pad.T)
    expand = (lax.broadcasted_iota(jnp.int32, (rows, LANE), 1) == (row_i // hp) * hp).astype(BF16)
    sel64 = _bf(_dot(expand, sel_rows))

    n_new = 16
    new_all = new_ref[...]
    new_sk = _bf(new_all[:, KV_ROWS:KV_ROWS + K_ROWS])
    new_sv = _bf(new_all[:, KV_ROWS + K_ROWS:2 * KV_ROWS])
    new_wk = _bf(new_all[:, 2 * KV_ROWS:2 * KV_ROWS + K_ROWS])
    new_wv = _bf(new_all[:, 2 * KV_ROWS + K_ROWS:3 * KV_ROWS])
    new_i = lax.broadcasted_iota(jnp.int32, (1, n_new), 1)
    new_ok = (new_i <= row_s) & (new_i < n_s)

    for p in range(n_pages):
        s_scr[:, p * page:(p + 1) * page] = _dot(q_bd, _bf(page_refs[p][0:K_ROWS, :]))
    picked = _dot(sel64, e_ref[...])
    s_all = jnp.where(picked > 0.5, s_scr[...], NEG_BIG)
    s_new = jnp.where(new_ok, _dot_nt(q_bd, new_sk), NEG_BIG)
    m_s = jnp.maximum(jnp.max(s_all, axis=-1, keepdims=True), jnp.max(s_new, axis=-1, keepdims=True))
    p_all = jnp.exp(s_all - m_s)
    p_new = jnp.exp(s_new - m_s)
    l_s = jnp.sum(p_all, axis=-1, keepdims=True) + jnp.sum(p_new, axis=-1, keepdims=True)
    acc = _dot(_bf(p_new), new_sv)
    p_all = _bf(p_all)
    for p in range(n_pages):
        acc = acc + _dot_nt(p_all[:, p * page:(p + 1) * page], _bf(page_refs[p][K_ROWS:KV_ROWS, :]))
    o_s = fold(acc) / l_s

    wlen = win_ref.shape[1]
    kk = lax.broadcasted_iota(jnp.int32, (1, wlen), 1)
    dpos = (past + row_s) - (past - wlen + kk)
    w_ok = (dpos >= 0) & (dpos < WINDOW)
    s_w = jnp.where(w_ok, _dot(q_bd, _bf(win_ref[0:K_ROWS, :])), NEG_BIG)
    s_wn = jnp.where(new_ok, _dot_nt(q_bd, new_wk), NEG_BIG)
    m_w = jnp.maximum(jnp.max(s_w, axis=-1, keepdims=True), jnp.max(s_wn, axis=-1, keepdims=True))
    p_w = jnp.exp(s_w - m_w)
    p_wn = jnp.exp(s_wn - m_w)
    l_w = jnp.sum(p_w, axis=-1, keepdims=True) + jnp.sum(p_wn, axis=-1, keepdims=True)
    acc_w = _dot_nt(_bf(p_w), _bf(win_ref[K_ROWS:KV_ROWS, :])) + _dot(_bf(p_wn), new_wv)
    o_w = fold(acc_w) / l_w

    gates = _sigmoid(gate_ref[...])
    o_ref[...] = gates[:, 0:1] * o_c + gates[:, 1:2] * o_s + gates[:, 2:3] * o_w


def _attn_sample(page_table, cache_pages, page_specs, q_s, win_state, new_kv, c_t, gates, e_mat, l,
                 n_b, n_s, past):
    n_pages = len(page_specs)
    page = LANE
    rows = N_KV * n_s * HEADS_PER_KV
    n_blk = c_t.shape[-1]
    wlen = win_state.shape[-1]
    body = functools.partial(_attn_sample_body, n_pages=n_pages, page=page, n_s=n_s, past=past)
    grid_spec = pltpu.PrefetchScalarGridSpec(
        num_scalar_prefetch=1,
        grid=(n_b,),
        in_specs=list(page_specs) + [
            pl.BlockSpec((None, rows, LANE), lambda b, pt: (b, 0, 0)),
            pl.BlockSpec((None, KV_ROWS, wlen), lambda b, pt: (l * n_b + b, 0, 0)),
            pl.BlockSpec((None, new_kv.shape[1], KV_W), lambda b, pt: (b, 0, 0)),
            pl.BlockSpec((None, 2, K_ROWS, n_blk), lambda b, pt: (b, 0, 0, 0)),
            pl.BlockSpec((None, rows, 3), lambda b, pt: (b, 0, 0)),
            pl.BlockSpec((LANE, n_pages * page), lambda b, pt: (0, 0)),
        ],
        out_specs=pl.BlockSpec((None, rows, LANE), lambda b, pt: (b, 0, 0)),
        scratch_shapes=[
            pltpu.VMEM((rows, n_pages * page), F32),
            pltpu.VMEM((176, LANE), F32),
        ],
    )
    return pl.pallas_call(
        body,
        grid_spec=grid_spec,
        out_shape=jax.ShapeDtypeStruct((n_b, rows, LANE), F32),
        compiler_params=_params(("arbitrary",)),
        name="attn_sample",
    )(page_table, *([cache_pages] * n_pages), q_s, win_state, new_kv, c_t, gates, e_mat)


def _merge_body(x_ref, m_ref, y0_ref, y1_ref, y2_ref, gm_ref, w_ref, g_ref, b_ref, o_ref, acc_scr,
                *, per_row, tiles_per_b, reps, alpha):
    i = pl.program_id(0)
    k = pl.program_id(1)

    def branch(y_ref, idx):
        contrib = _sigmoid(gm_ref[...]) * _dot(_bf(y_ref[...]), w_ref[idx])
        if idx == 0:
            acc_scr[...] = contrib
        else:
            acc_scr[...] += contrib

    @pl.when(k == 0)
    def _():
        branch(y0_ref, 0)

    @pl.when(k == 1)
    def _():
        branch(y1_ref, 1)

    @pl.when(k == 2)
    def _():
        branch(y2_ref, 2)

    @pl.when(k == 3)
    def _():
        out = _dot(_bf(acc_scr[...]), w_ref[3])
        gate = _mod_rows(m_ref, 2, per_row, i // tiles_per_b, reps)
        z = alpha * x_ref[...] + gate * out
        o_ref[...] = _layer_norm(z, g_ref[...], b_ref[...])


def _merge(x, mods, l, y_rnn, y_pool, y_attn, proj_a, w_stack, ln_g, ln_b, *, per_row, tm, rows_per_b,
           mod_row_block, mod_rows, alpha):
    m_tok = x.shape[0]
    reps = tm // mod_rows if per_row else 1
    tiles_per_b = max(rows_per_b // tm, 1)
    body = functools.partial(_merge_body, per_row=per_row, tiles_per_b=tiles_per_b, reps=reps, alpha=alpha)
    y_spec = pl.BlockSpec((tm, D_MODEL), lambda i, k: (i, 0))
    return pl.pallas_call(
        body,
        grid=(m_tok // tm, 4),
        in_specs=[
            y_spec,
            pl.BlockSpec((None, 3, mod_rows, D_MODEL), lambda i, k: (l, 1, mod_row_block, 0)),
            y_spec, y_spec, y_spec,
            pl.BlockSpec((tm, D_MODEL), lambda i, k: (i, 3 + jnp.minimum(k, 2))),
            pl.BlockSpec((None, 4, D_MODEL, D_MODEL), lambda i, k: (l, 0, 0, 0)),
            pl.BlockSpec((None, None, 1, D_MODEL), lambda i, k: (l, 1, 0, 0)),
            pl.BlockSpec((None, None, 1, D_MODEL), lambda i, k: (l, 1, 0, 0)),
        ],
        out_specs=y_spec,
        out_shape=jax.ShapeDtypeStruct((m_tok, D_MODEL), F32),
        scratch_shapes=[pltpu.VMEM((tm, D_MODEL), F32)],
        compiler_params=_params(("arbitrary", "arbitrary")),
        name="merge",
    )(x, mods, y_rnn, y_pool, y_attn, proj_a, w_stack, ln_g, ln_b)


def _rope_tables(pos):
    inv = ROPE_THETA ** (-jnp.arange(ROT_HALF, dtype=F32) * (2.0 / ROT_DIM))
    ang = inv[:, None] * pos.astype(F32)[None, :]
    return jnp.cos(ang), jnp.sin(ang)


def _rope_lane_tables(cos, sin):
    d = np.arange(LANE) % HEAD_DIM
    idx = np.where(d < ROT_HALF, d, np.clip(d - ROT_HALF, 0, ROT_HALF - 1))
    cos_l = cos.T[:, idx]
    sin_l = sin.T[:, idx]
    first = jnp.asarray(d < ROT_HALF)[None, :]
    second = jnp.asarray((d >= ROT_HALF) & (d < ROT_DIM))[None, :]
    c_tab = jnp.where(first | second, cos_l, 1.0)
    s_lo = jnp.where(second, sin_l, 0.0)
    s_hi = jnp.where(first, -sin_l, 0.0)
    return c_tab, s_lo, s_hi


def kernel(x_prompt, x_sample, c_prompt, c_sample, cache_cmp_kv, cache_slc_kv, state_win_kv, state_rnn_h, state_rnn_conv, state_pool, page_table, ada_w, ada_b, ln_g, ln_b, ffn_w_up, ffn_w_down, w_in, b_in, rnn_conv_w, rnn_conv_b, rnn_gate_w, rnn_gate_b, rnn_lambda, pool_w, pool_scale, cmp_w1, cmp_b1, cmp_w2, cmp_pe, w_rnn_out, w_pool_out, w_attn_out, w_out):
    depth = ada_w.shape[0]
    n_bp, seq, _ = x_prompt.shape
    n_bs, n_s, _ = x_sample.shape
    n_pool, page = cache_cmp_kv.shape[1], cache_cmp_kv.shape[2]
    n_pages = page_table.shape[1]
    past = n_pages * page
    wlen = state_win_kv.shape[2]
    alpha = (2 * depth) ** 0.25
    assert page == LANE and seq == n_pages * page and n_bs == LANE

    c_all = jnp.concatenate([c_sample, c_prompt], axis=0)
    mods = _adaln(c_all, ada_w, ada_b)
    mod_p = dict(per_row=False, rows_per_b=seq, mod_row_block=n_bs // n_bp, mod_rows=n_bp)
    mod_s = dict(per_row=True, rows_per_b=n_bs * n_s, mod_row_block=0, mod_rows=n_bs)
    ln_g4 = ln_g.reshape(depth, 3, 1, D_MODEL)
    ln_b4 = ln_b.reshape(depth, 3, 1, D_MODEL)
    wt_full = jnp.swapaxes(w_in, 1, 2)
    c0 = 3 * D_MODEL + ATTN_W + KV_W
    c1 = c0 + 3 * N_HEADS
    pad_cols = PROJ_A_COLS - 6 * D_MODEL - 3 * N_HEADS
    wt_a = _bf(jnp.concatenate([wt_full[:, :3 * D_MODEL], wt_full[:, c1:], wt_full[:, c0:c1],
                                jnp.zeros((depth, pad_cols, D_MODEL), F32)], axis=1))
    ffn_up_b = _bf(ffn_w_up)
    ffn_down_b = _bf(ffn_w_down)
    bias_a = jnp.concatenate([b_in[:, :3 * D_MODEL], b_in[:, c1:], b_in[:, c0:c1],
                              jnp.zeros((depth, pad_cols), F32)], axis=1).reshape(depth, 1, PROJ_A_COLS)
    q0c = 3 * D_MODEL
    bias_q = b_in[:, q0c:q0c + ATTN_W].reshape(depth, 1, ATTN_W)
    bias_b = b_in[:, q0c + ATTN_W:q0c + ATTN_W + KV_W].reshape(depth, KV_W, 1)
    conv_b3 = rnn_conv_b.reshape(depth, 1, D_RNN)
    lam3 = rnn_lambda.reshape(depth, 1, D_RNN)
    per = 256 // RNN_BW
    gw = rnn_gate_w.reshape(depth, 2, D_RNN // 256, per, RNN_BW, RNN_BW)
    eye = jnp.eye(per, dtype=F32)
    wg = jnp.einsum('lcbpij,pq->lcbpiqj', gw, eye).reshape(depth, 2, D_RNN // 256, 256, 256)
    pool_scale3 = pool_scale.reshape(depth, 1, D_POOL)
    half = CMP_STRIDE * HEAD_DIM
    w1c = jnp.concatenate([cmp_w1[:, :, :half, :], cmp_w1[:, :, half:, :]], axis=-1)
    w2t = jnp.swapaxes(cmp_w2, 2, 3)
    pe_term = _pe_term(cmp_pe, cmp_w1, cmp_b1)
    w_stack = _bf(jnp.stack([w_rnn_out, w_pool_out, w_attn_out, w_out], axis=1))
    e_mat = (jnp.arange(LANE, dtype=jnp.int32)[:, None]
             == (jnp.arange(seq, dtype=jnp.int32)[None, :] // SLC_BLOCK)).astype(BF16)
    cos_p, sin_p = _rope_tables(jnp.arange(seq, dtype=jnp.int32))
    pos_s = past + jnp.repeat(jnp.arange(n_s, dtype=jnp.int32), n_bs)
    cos_s, sin_s = _rope_tables(pos_s)
    qtab_p = _rope_lane_tables(cos_p, sin_p)
    qtab_s = _rope_lane_tables(cos_s, sin_s)

    def kv_t(a):
        lead = a.shape[:-4]
        nd = len(lead)
        a = jnp.transpose(a, tuple(range(nd)) + (nd + 1, nd + 2, nd + 3, nd))
        return a.reshape(lead + (KV_ROWS, a.shape[-1]))

    cache_cmp_t = kv_t(cache_cmp_kv).reshape(depth * n_pool, KV_ROWS, page)
    cache_slc_t = kv_t(cache_slc_kv).reshape(depth * n_pool, KV_ROWS, page)
    win_t = kv_t(state_win_kv)
    win_flat = win_t.reshape(depth * n_bs, KV_ROWS, wlen)
    conv_state = jnp.swapaxes(state_rnn_conv, 1, 2)
    pool_state = jnp.swapaxes(state_pool, 1, 2)
    page_ident = jnp.arange(n_bp * n_pages, dtype=jnp.int32).reshape(n_bp, n_pages) % n_pages

    xp = x_prompt.reshape(n_bp * seq, D_MODEL)
    xs = jnp.swapaxes(x_sample, 0, 1).reshape(n_s * n_bs, D_MODEL)
    tm_p = 1024
    tm_s = n_s * n_bs

    outs = {k: [] for k in ("cmp_p", "cmp_s", "slc_p", "slc_s", "win_p", "win_s", "h_p", "h_s",
                            "conv_p", "conv_s", "pool_p", "pool_s")}
    for l in range(depth):
        xp = _ffn(xp, mods, l, 0, ffn_up_b, ffn_down_b, ln_g4, ln_b4, 0, tm=tm_p, alpha=alpha, **mod_p)
        xs = _ffn(xs, mods, l, 0, ffn_up_b, ffn_down_b, ln_g4, ln_b4, 0, tm=tm_s, alpha=alpha, **mod_s)

        pa_p = _proj_a(xp, mods, l, wt_a, bias_a, tm=tm_p, **mod_p)
        pa_s = _proj_a(xs, mods, l, wt_a, bias_a, tm=tm_s, **mod_s)
        kv_p = _proj_b(xp, mods, l, wt_full, bias_b, cos_p, sin_p, tm=tm_p, n_sub=1, **mod_p)
        kv_s = _proj_b(xs, mods, l, wt_full, bias_b, cos_s, sin_s, tm=tm_s, n_sub=n_s, **mod_s)
        q_p = _proj_q(xp, mods, l, wt_full, bias_q, qtab_p, tm=tm_p, **mod_p)
        q_s = _proj_q(xs, mods, l, wt_full, bias_q, qtab_s, tm=tm_s, **mod_s)

        y_rnn_p, hl_p, ul_p = _rglru_prompt(pa_p, l, n_bp, seq, rnn_conv_w, conv_b3, wg, rnn_gate_b, lam3)
        y_rnn_s, hn_s, cn_s = _rglru_sample(pa_s, l, n_bs, n_s, conv_state, state_rnn_h, rnn_conv_w,
                                            conv_b3, wg, rnn_gate_b, lam3)
        y_pool_p, last_p = _pool_prompt(pa_p, l, n_bp, seq, pool_w, pool_scale3)
        y_pool_s, pn_s = _pool_sample(pa_s, l, n_bs, n_s, pool_state, pool_w, pool_scale3)

        c_p = _compress(page_ident, kv_p, lambda b, pt, p: (b, 0, pt[b, p]), n_pages, n_bp, l,
                        w1c, pe_term, w2t)
        c_s = _compress(page_table, cache_cmp_t, lambda b, pt, p, l=l: (l * n_pool + pt[b, p], 0, 0),
                        n_pages, n_bs, l, w1c, pe_term, w2t)

        y_attn_p = _attn_prompt_km(q_p, kv_p, c_p, pa_p, n_bp, seq)
        rows_s = N_KV * n_s * HEADS_PER_KV
        q_b = q_s.reshape(n_s, n_bs, N_KV, HEADS_PER_KV, HEAD_DIM)
        q_b = jnp.transpose(q_b, (1, 2, 0, 3, 4)).reshape(n_bs, rows_s, HEAD_DIM)
        q_b = jnp.concatenate([q_b, q_b], axis=-1)
        new_kv = jnp.transpose(kv_s, (2, 0, 1))
        new_kv = jnp.pad(new_kv, ((0, 0), (0, 16 - n_s), (0, 0)))
        g_s = pa_s[:, 6 * D_MODEL:6 * D_MODEL + 3 * N_HEADS].reshape(n_s, n_bs, N_KV, HEADS_PER_KV, 3)
        g_s = jnp.transpose(g_s, (1, 2, 0, 3, 4)).reshape(n_bs, rows_s, 3)
        slc_specs = [pl.BlockSpec((None, KV_ROWS, page), functools.partial(
            lambda b, pt, p: (l * n_pool + pt[b, p], 0, 0), p=p)) for p in range(n_pages)]
        o_s = _attn_sample(page_table, cache_slc_t, slc_specs, q_b, win_flat, new_kv, c_s, g_s, e_mat, l,
                           n_bs, n_s, past)
        y_attn_s = o_s[:, :, :HEAD_DIM].reshape(n_bs, N_KV, n_s, HEADS_PER_KV, HEAD_DIM)
        y_attn_s = jnp.transpose(y_attn_s, (2, 0, 1, 3, 4)).reshape(n_s * n_bs, ATTN_W)

        xp = _merge(xp, mods, l, y_rnn_p, y_pool_p, y_attn_p, pa_p, w_stack, ln_g4, ln_b4, tm=512,
                    alpha=alpha, **mod_p)
        xs = _merge(xs, mods, l, y_rnn_s, y_pool_s, y_attn_s, pa_s, w_stack, ln_g4, ln_b4, tm=tm_s,
                    alpha=alpha, **mod_s)

        xp = _ffn(xp, mods, l, 2, ffn_up_b, ffn_down_b, ln_g4, ln_b4, 2, tm=tm_p, alpha=alpha, **mod_p)
        xs = _ffn(xs, mods, l, 2, ffn_up_b, ffn_down_b, ln_g4, ln_b4, 2, tm=tm_s, alpha=alpha, **mod_s)

        kvp = kv_p.reshape(n_bp, 3, 2, N_KV, HEAD_DIM, seq)
        kvs = kv_s.reshape(n_s, 3, 2, N_KV, HEAD_DIM, n_bs)
        outs["cmp_p"].append(kvp[:, 0])
        outs["slc_p"].append(kvp[:, 1])
        outs["win_p"].append(kvp[:, 2, :, :, :, seq - min(WINDOW, seq):])
        outs["cmp_s"].append(kvs[:, 0])
        outs["slc_s"].append(kvs[:, 1])
        outs["win_s"].append(jnp.transpose(kvs[:, 2], (4, 1, 2, 3, 0)))
        outs["h_p"].append(hl_p.reshape(n_bp, D_RNN))
        outs["h_s"].append(hn_s)
        outs["conv_p"].append(jnp.swapaxes(ul_p[:, 8 - (CONV_W - 1):, :], 0, 1))
        outs["conv_s"].append(cn_s)
        outs["pool_p"].append(jnp.swapaxes(last_p[:, 16 - POOL_BUF:, :], 0, 1))
        outs["pool_s"].append(pn_s)

    def kv_out_p(lst):
        return jnp.transpose(jnp.stack(lst), (0, 1, 5, 2, 3, 4))

    def kv_out_s(lst):
        return jnp.transpose(jnp.stack(lst), (0, 5, 1, 2, 3, 4))

    win_old = win_t.reshape(depth, n_bs, 2, N_KV, HEAD_DIM, wlen)
    win_s_t = jnp.concatenate([win_old[..., n_s:], jnp.stack(outs["win_s"])], axis=-1)
    win_s_out = jnp.transpose(win_s_t, (0, 1, 5, 2, 3, 4))

    y_prompt = xp.reshape(n_bp, seq, D_MODEL)
    y_sample = jnp.swapaxes(xs.reshape(n_s, n_bs, D_MODEL), 0, 1)
    return (y_prompt, y_sample,
            kv_out_p(outs["cmp_p"]), kv_out_s(outs["cmp_s"]),
            kv_out_p(outs["slc_p"]), kv_out_s(outs["slc_s"]),
            kv_out_p(outs["win_p"]), win_s_out,
            jnp.stack(outs["h_p"]), jnp.stack(outs["h_s"]),
            jnp.swapaxes(jnp.stack(outs["conv_p"]), 1, 2), jnp.swapaxes(jnp.stack(outs["conv_s"]), 1, 2),
            jnp.swapaxes(jnp.stack(outs["pool_p"]), 1, 2), jnp.swapaxes(jnp.stack(outs["pool_s"]), 1, 2))
```
